```python
import math
import jax, jax.numpy as jnp
from jax import lax
import numpy as np

D_MODEL = 1024
BATCH = 8
SEQ = 2048
DEPTH = 4

CHUNK = 64
N_MIXERS = 2
HEAD_DIM_GDN = 128
N_HEADS_GDN = D_MODEL // HEAD_DIM_GDN
GDN_DIM = N_HEADS_GDN * HEAD_DIM_GDN
CONV_K = 4
HEAD_DIM_FOX = 64
N_HEADS_FOX = D_MODEL // HEAD_DIM_FOX
FOX_DIM = N_HEADS_FOX * HEAD_DIM_FOX
Q_BLOCK = 128
N_GROUPS = 4
EXPERTS_PER_GROUP = 8
TOP_K = 2
D_EXPERT = D_MODEL // 4
EPS = 1e-6

kernel_name = 'hybrid_gdn_fox_hmoe_adaln'


def rms_norm(x, gain=None):
    x32 = x.astype(jnp.float32)
    y = x32 * lax.rsqrt(jnp.mean(x32 * x32, axis=-1, keepdims=True) + EPS)
    if gain is not None:
        y = y * gain.astype(jnp.float32)
    return y.astype(x.dtype)


def l2_norm(x):
    x32 = x.astype(jnp.float32)
    return x32 * lax.rsqrt(jnp.sum(x32 * x32, axis=-1, keepdims=True) + EPS)


def causal_depthwise_conv(x, w):
    C = x.shape[-1]
    return lax.conv_general_dilated(
        x, w[:, None, :].astype(x.dtype), window_strides=(1,), padding=[(CONV_K - 1, 0)],
        dimension_numbers=('NWC', 'WIO', 'NWC'), feature_group_count=C)


def gated_delta_rule(q, k, v, log_alpha, beta):
    f32 = jnp.float32
    B, H, L, Dk = q.shape
    Dv = v.shape[-1]
    n = L // CHUNK
    q = (q.astype(f32) * Dk ** -0.5).reshape(B, H, n, CHUNK, Dk)
    k = k.astype(f32).reshape(B, H, n, CHUNK, Dk)
    v = v.astype(f32).reshape(B, H, n, CHUNK, Dv)
    beta = beta.astype(f32).reshape(B, H, n, CHUNK, 1)
    g = jnp.cumsum(log_alpha.astype(f32).reshape(B, H, n, CHUNK), axis=-1)
    diff = g[..., :, None] - g[..., None, :]
    incl = jnp.tril(jnp.ones((CHUNK, CHUNK), dtype=bool))
    strict = jnp.tril(jnp.ones((CHUNK, CHUNK), dtype=bool), -1)
    decay_incl = jnp.exp(jnp.where(incl, diff, -jnp.inf))
    decay_strict = jnp.where(strict, decay_incl, 0.0)
    kb = k * beta
    a_mat = jnp.einsum('bhnik,bhnjk->bhnij', kb, k) * decay_strict + jnp.eye(CHUNK, dtype=f32)
    u = lax.linalg.triangular_solve(a_mat, v * beta, left_side=True, lower=True)
    w = lax.linalg.triangular_solve(a_mat, kb * jnp.exp(g)[..., None], left_side=True, lower=True)
    qk = jnp.einsum('bhnik,bhnjk->bhnij', q, k) * decay_incl
    qg = q * jnp.exp(g)[..., None]
    g_last = g[..., -1]
    kd = k * jnp.exp(g_last[..., None] - g)[..., None]

    def step(S, xs):
        u_c, w_c, qg_c, qk_c, kd_c, gl_c = xs
        v_new = u_c - jnp.einsum('bhck,bhkv->bhcv', w_c, S)
        o_c = jnp.einsum('bhck,bhkv->bhcv', qg_c, S) + jnp.einsum('bhcs,bhsv->bhcv', qk_c, v_new)
        S = S * jnp.exp(gl_c)[..., None, None] + jnp.einsum('bhck,bhcv->bhkv', kd_c, v_new)
        return S, o_c

    xs = (jnp.moveaxis(u, 2, 0), jnp.moveaxis(w, 2, 0), jnp.moveaxis(qg, 2, 0),
          jnp.moveaxis(qk, 2, 0), jnp.moveaxis(kd, 2, 0), jnp.moveaxis(g_last, 2, 0))
    S0 = jnp.zeros((B, H, Dk, Dv), f32)
    _, o = lax.scan(step, S0, xs)
    return jnp.moveaxis(o, 0, 2).reshape(B, H, L, Dv)


def gdn_mixer(h, w_in, conv_w, a_log, dt_bias, o_gain, w_out):
    B, L, _ = h.shape
    proj = h @ w_in
    qkv, gate, a, b = jnp.split(proj, [3 * GDN_DIM, 4 * GDN_DIM, 4 * GDN_DIM + N_HEADS_GDN], axis=-1)
    qkv = jax.nn.silu(causal_depthwise_conv(qkv, conv_w))
    q, k, v = jnp.split(qkv, 3, axis=-1)
    to_heads = lambda t: t.reshape(B, L, N_HEADS_GDN, HEAD_DIM_GDN).transpose(0, 2, 1, 3)
    q = l2_norm(to_heads(q))
    k = l2_norm(to_heads(k))
    v = to_heads(v)
    beta = jax.nn.sigmoid(b.astype(jnp.float32)).transpose(0, 2, 1)
    log_alpha = (-jnp.exp(a_log.astype(jnp.float32))
                 * jax.nn.softplus(a.astype(jnp.float32) + dt_bias.astype(jnp.float32))).transpose(0, 2, 1)
    o = gated_delta_rule(q, k, v, log_alpha, beta).astype(h.dtype)
    o = o.transpose(0, 2, 1, 3)
    o = rms_norm(o, o_gain) * jax.nn.silu(gate.reshape(B, L, N_HEADS_GDN, HEAD_DIM_GDN))
    return o.reshape(B, L, GDN_DIM) @ w_out


def fox_mixer(h, w_in, b_f, q_gain, k_gain, w_out):
    B, L, _ = h.shape
    proj = h @ w_in
    q, k, v, g_out, f_logit = jnp.split(proj, [FOX_DIM, 2 * FOX_DIM, 3 * FOX_DIM, 4 * FOX_DIM], axis=-1)
    to_heads = lambda t: t.reshape(B, L, N_HEADS_FOX, HEAD_DIM_FOX)
    q = rms_norm(to_heads(q), q_gain).transpose(0, 2, 1, 3)
    k = rms_norm(to_heads(k), k_gain).transpose(0, 2, 1, 3)
    v = to_heads(v).transpose(0, 2, 1, 3)
    log_f = jax.nn.log_sigmoid(f_logit.astype(jnp.float32) + b_f.astype(jnp.float32))
    cum = jnp.cumsum(log_f, axis=1).transpose(0, 2, 1)
    scale = HEAD_DIM_FOX ** -0.5
    outs = []
    for blk in range(L // Q_BLOCK):
        q0 = blk * Q_BLOCK
        q1 = q0 + Q_BLOCK
        s = jnp.einsum('bhqd,bhkd->bhqk', q[:, :, q0:q1], k[:, :, :q1]).astype(jnp.float32) * scale
        s = s + cum[:, :, q0:q1, None] - cum[:, :, None, :q1]
        mask = jnp.arange(q1)[None, :] <= jnp.arange(q0, q1)[:, None]
        p = jax.nn.softmax(jnp.where(mask, s, -jnp.inf), axis=-1).astype(v.dtype)
        outs.append(jnp.einsum('bhqk,bhkd->bhqd', p, v[:, :, :q1]))
    o = jnp.concatenate(outs, axis=2).transpose(0, 2, 1, 3).reshape(B, L, FOX_DIM)
    o = o * jax.nn.sigmoid(g_out)
    return o @ w_out


def hier_moe(h, w_group, b_group, w_router, b_router, w_gate_up, w_down):
    B, L, D = h.shape
    t = h.reshape(-1, D)
    T = t.shape[0]
    f32 = jnp.float32
    g_prob = jax.nn.softmax((t @ w_group + b_group).astype(f32), axis=-1)
    g_p, g_idx = lax.top_k(g_prob, 1)
    e_logits = (t @ w_router + b_router).astype(f32).reshape(T, N_GROUPS, EXPERTS_PER_GROUP)
    e_sel = jnp.take_along_axis(e_logits, g_idx[:, :, None], axis=1)[:, 0]
    e_top, e_idx = lax.top_k(e_sel, TOP_K)
    e_w = jax.nn.softmax(e_top, axis=-1) * g_p
    e_comb = jnp.einsum('tk,tke->te', e_w, jax.nn.one_hot(e_idx, EXPERTS_PER_GROUP, dtype=f32))
    combine = (jax.nn.one_hot(g_idx[:, 0], N_GROUPS, dtype=f32)[:, :, None]
               * e_comb[:, None, :]).astype(h.dtype)
    y = jnp.zeros_like(t)
    for gi in range(N_GROUPS):
        hu = jnp.einsum('td,edf->tef', t, w_gate_up[gi])
        gate, up = jnp.split(hu, 2, axis=-1)
        act = jax.nn.silu(gate) * up * combine[:, gi, :, None]
        y = y + jnp.einsum('tef,efd->td', act, w_down[gi])
    return y.reshape(B, L, D)


def setup_inputs(seed: int = 0) -> dict:
    key = jax.random.key(seed)
    ks = jax.random.split(key, 24)
    nrm = lambda k, shape, s: jax.random.normal(k, shape, jnp.float32) * s
    n_a = (DEPTH + 1) // N_MIXERS
    n_b = DEPTH // N_MIXERS
    gdn_in = 4 * GDN_DIM + 2 * N_HEADS_GDN
    fox_in = 4 * FOX_DIM + N_HEADS_FOX
    dt = jnp.exp(jax.random.uniform(ks[7], (n_a, N_HEADS_GDN), jnp.float32,
                                    minval=math.log(1e-3), maxval=math.log(1e-1)))
    return {
        'x': nrm(ks[0], (BATCH, SEQ, D_MODEL), 1.0),
        'c': nrm(ks[1], (BATCH, D_MODEL), 1.0),
        'ada_w': nrm(ks[2], (DEPTH, D_MODEL, 6 * D_MODEL), 0.5 * D_MODEL ** -0.5),
        'ada_b': nrm(ks[3], (DEPTH, 6 * D_MODEL), 0.02),
        'gdn_w_in': nrm(ks[4], (n_a, D_MODEL, gdn_in), D_MODEL ** -0.5),
        'gdn_conv': nrm(ks[5], (n_a, CONV_K, 3 * GDN_DIM), CONV_K ** -0.5),
        'gdn_a_log': jnp.log(jax.random.uniform(ks[6], (n_a, N_HEADS_GDN), jnp.float32, minval=1.0, maxval=16.0)),
        'gdn_dt_bias': dt + jnp.log(-jnp.expm1(-dt)),
        'gdn_norm': 1.0 + nrm(ks[8], (n_a, HEAD_DIM_GDN), 0.02),
        'gdn_w_out': nrm(ks[9], (n_a, GDN_DIM, D_MODEL), GDN_DIM ** -0.5),
        'fox_w_in': nrm(ks[10], (n_b, D_MODEL, fox_in), D_MODEL ** -0.5),
        'fox_b_f': 3.0 + nrm(ks[11], (n_b, N_HEADS_FOX), 0.5),
        'fox_q_norm': 1.0 + nrm(ks[12], (n_b, HEAD_DIM_FOX), 0.02),
        'fox_k_norm': 1.0 + nrm(ks[13], (n_b, HEAD_DIM_FOX), 0.02),
        'fox_w_out': nrm(ks[14], (n_b, FOX_DIM, D_MODEL), FOX_DIM ** -0.5),
        'moe_w_group': nrm(ks[15], (DEPTH, D_MODEL, N_GROUPS), D_MODEL ** -0.5),
        'moe_b_group': nrm(ks[16], (DEPTH, N_GROUPS), 0.01),
        'moe_w_router': nrm(ks[17], (DEPTH, D_MODEL, N_GROUPS * EXPERTS_PER_GROUP), D_MODEL ** -0.5),
        'moe_b_router': nrm(ks[18], (DEPTH, N_GROUPS * EXPERTS_PER_GROUP), 0.01),
        'moe_w_gate_up': nrm(ks[19], (DEPTH, N_GROUPS, EXPERTS_PER_GROUP, D_MODEL, 2 * D_EXPERT), D_MODEL ** -0.5),
        'moe_w_down': nrm(ks[20], (DEPTH, N_GROUPS, EXPERTS_PER_GROUP, D_EXPERT, D_MODEL), D_EXPERT ** -0.5),
        'final_norm': 1.0 + nrm(ks[21], (D_MODEL,), 0.02),
    }


def reference(x, c, ada_w, ada_b, gdn_w_in, gdn_conv, gdn_a_log, gdn_dt_bias, gdn_norm, gdn_w_out,
              fox_w_in, fox_b_f, fox_q_norm, fox_k_norm, fox_w_out,
              moe_w_group, moe_b_group, moe_w_router, moe_b_router, moe_w_gate_up, moe_w_down,
              final_norm):
    c_act = jax.nn.silu(c)
    for i in range(DEPTH):
        ada = (c_act @ ada_w[i] + ada_b[i])[:, None, :]
        sh1, sc1, g1, sh2, sc2, g2 = jnp.split(ada, 6, axis=-1)
        h = rms_norm(x) * (1.0 + sc1) + sh1
        j = i // N_MIXERS
        if i % N_MIXERS == 0:
            y = gdn_mixer(h, gdn_w_in[j], gdn_conv[j], gdn_a_log[j], gdn_dt_bias[j], gdn_norm[j], gdn_w_out[j])
        else:
            y = fox_mixer(h, fox_w_in[j], fox_b_f[j], fox_q_norm[j], fox_k_norm[j], fox_w_out[j])
        x = x + g1 * y
        h = rms_norm(x) * (1.0 + sc2) + sh2
        x = x + g2 * hier_moe(h, moe_w_group[i], moe_b_group[i], moe_w_router[i], moe_b_router[i],
                              moe_w_gate_up[i], moe_w_down[i])
    return rms_norm(x, final_norm)
```

```python
import functools

import jax
import jax.numpy as jnp
from jax import lax
from jax.experimental import pallas as pl
from jax.experimental.pallas import tpu as pltpu

F32 = jnp.float32
BF16 = jnp.bfloat16

EPS = 1e-6
CHUNK = 64
N_HEADS_GDN = 8
HEAD_DIM_GDN = 128
N_HEADS_FOX = 16
HEAD_DIM_FOX = 64
CONV_K = 4
N_GROUPS = 4
EXPERTS_PER_GROUP = 8
N_EXPERTS = N_GROUPS * EXPERTS_PER_GROUP
TOP_K = 2

LANES = 128
SUBLANES = 8
VMEM_LIMIT_BYTES = 56 * 1024 * 1024

ROW_TILE = 512
ATTN_TILE = 256
GDN_BLOCK = 512
MOE_TILE = 256
TOKEN_TILE = 256
NEG_BIG = -1e30


def _params(semantics):
    return pltpu.CompilerParams(dimension_semantics=semantics, vmem_limit_bytes=VMEM_LIMIT_BYTES)


def _sigmoid(x):
    return 1.0 / (1.0 + jnp.exp(-x))


def _silu(x):
    return x * _sigmoid(x)


def _softplus(x):
    return jnp.maximum(x, 0.0) + jnp.log(1.0 + jnp.exp(-jnp.abs(x)))


def _split2(a):
    hi = a.astype(BF16)
    lo = (a - hi.astype(F32)).astype(BF16)
    return hi, lo


def _split3(a):
    hi = a.astype(BF16)
    r = a - hi.astype(F32)
    mid = r.astype(BF16)
    lo = (r - mid.astype(F32)).astype(BF16)
    return hi, mid, lo


def _dot(a, b):
    return jnp.dot(a, b, preferred_element_type=F32)


def _dot_nt(a, b):
    return lax.dot_general(a, b, (((1,), (1,)), ((), ())), preferred_element_type=F32)


def _dot_split_weights(a, w_hi_lo, n):
    a_hi, a_lo = _split2(a)
    r = _dot(a_hi, w_hi_lo)
    return r[:, :n] + r[:, n:] + _dot(a_lo, w_hi_lo[:, :n])


def _cumsum_rows(tri_bf16, v):
    hi, mid, lo = _split3(v)
    r = _dot(tri_bf16, jnp.concatenate([hi, mid], axis=1))
    return r[:, :LANES] + r[:, LANES:] + _dot(tri_bf16, lo)


def _rms_mod(x, sc, sh):
    ms = jnp.mean(x * x, axis=-1, keepdims=True)
    return x * lax.rsqrt(ms + EPS) * (1.0 + sc) + sh


def _lane_col(v, idx, lane):
    return jnp.sum(jnp.where(lane == idx, v, 0.0), axis=1, keepdims=True)


def _ada_kernel(c_ref, w_ref, b_ref, o_ref):
    c = c_ref[...]
    o_ref[0] = _dot(_silu(c).astype(BF16), w_ref[0].astype(BF16)) + b_ref[0]


def _ada(c, ada_w, ada_b):
    depth, d, n = ada_w.shape
    b = c.shape[0]
    tn = 1536
    return pl.pallas_call(
        _ada_kernel,
        grid=(depth, n // tn),
        in_specs=[
            pl.BlockSpec((b, d), lambda i, j: (0, 0)),
            pl.BlockSpec((1, d, tn), lambda i, j: (i, 0, j)),
            pl.BlockSpec((1, 1, tn), lambda i, j: (i, 0, j)),
        ],
        out_specs=pl.BlockSpec((1, b, tn), lambda i, j: (i, 0, j)),
        out_shape=jax.ShapeDtypeStruct((depth, b, n), F32),
        compiler_params=_params(("arbitrary", "arbitrary")),
        name="ada",
    )(c, ada_w, ada_b.reshape(depth, 1, n))


def _gdn_inproj_kernel(x_ref, sc_ref, sh_ref, w_ref, wab_ref, conv_ref, prm_ref,
                       q_ref, k_ref, v_ref, gate_ref, gb_ref, halo_ref):
    tm = x_ref.shape[1]
    d = x_ref.shape[2]
    h = _rms_mod(x_ref[0], sc_ref[0], sh_ref[0])
    hb = h.astype(BF16)

    @pl.when(pl.program_id(1) == 0)
    def _():
        halo_ref[:, 0:SUBLANES, :] = jnp.zeros((3, SUBLANES, d), F32)

    for s, o_ref in enumerate((q_ref, k_ref, v_ref)):
        raw = _dot(hb, w_ref[:, s * d:(s + 1) * d])
        halo_ref[s, SUBLANES:SUBLANES + tm, :] = raw
        cw = conv_ref[:, s * d:(s + 1) * d]
        y = raw * cw[3:4]
        for j in range(CONV_K - 1):
            off = SUBLANES - (CONV_K - 1) + j
            y = y + halo_ref[s, off:off + tm, :] * cw[j:j + 1]
        halo_ref[s, 0:SUBLANES, :] = halo_ref[s, tm:tm + SUBLANES, :]
        y = _silu(y)
        if s < 2:
            scale = HEAD_DIM_GDN ** -0.5 if s == 0 else 1.0
            for hh in range(N_HEADS_GDN):
                seg = y[:, hh * HEAD_DIM_GDN:(hh + 1) * HEAD_DIM_GDN]
                inv = lax.rsqrt(jnp.sum(seg * seg, axis=-1, keepdims=True) + EPS) * scale
                o_ref[0, :, hh * HEAD_DIM_GDN:(hh + 1) * HEAD_DIM_GDN] = (seg * inv).astype(o_ref.dtype)
        else:
            o_ref[0] = y.astype(o_ref.dtype)

    gate_ref[0] = _silu(_dot(hb, w_ref[:, 3 * d:4 * d])).astype(gate_ref.dtype)

    ab = _dot_split_weights(h, wab_ref[...], LANES)
    log_alpha = -jnp.exp(prm_ref[0:1, :]) * _softplus(ab + prm_ref[1:2, :])
    lane = lax.broadcasted_iota(jnp.int32, ab.shape, 1)
    gb_ref[0] = jnp.where(lane < N_HEADS_GDN, log_alpha, _sigmoid(ab))


def _gdn_inproj(x, sc, sh, w_main, w_ab, conv_w, prm):
    b, l, d = x.shape
    tm = min(ROW_TILE, l)
    act = jax.ShapeDtypeStruct((b, l, d), BF16)
    row = lambda i, j: (i, j, 0)
    vec = lambda i, j: (i, 0, 0)
    const = lambda i, j: (0, 0)
    return pl.pallas_call(
        _gdn_inproj_kernel,
        grid=(b, l // tm),
        in_specs=[
            pl.BlockSpec((1, tm, d), row),
            pl.BlockSpec((1, 1, d), vec),
            pl.BlockSpec((1, 1, d), vec),
            pl.BlockSpec(w_main.shape, const),
            pl.BlockSpec(w_ab.shape, const),
            pl.BlockSpec(conv_w.shape, const),
            pl.BlockSpec(prm.shape, const),
        ],
        out_specs=[pl.BlockSpec((1, tm, d), row)] * 4 + [pl.BlockSpec((1, tm, LANES), row)],
        out_shape=[act, act, act, act, jax.ShapeDtypeStruct((b, l, LANES), F32)],
        scratch_shapes=[pltpu.VMEM((3, tm + SUBLANES, d), F32)],
        compiler_params=_params(("arbitrary", "arbitrary")),
        name="gdn_inproj",
    )(x, sc, sh, w_main, w_ab, conv_w, prm)


def _mm3_rows(lhs, rhs):
    m = lhs.shape[0]
    l_hi, l_lo = _split2(lhs)
    r_hi, r_lo = _split2(rhs)
    t = _dot(jnp.concatenate([l_hi, l_lo], axis=0), r_hi)
    return t[:m] + t[m:] + _dot(l_hi, r_lo)


def _unit_lower_inverse(a_strict, eye):
    n = a_strict.shape[0]
    m = -a_strict
    x = eye + m
    p = _mm3_rows(m, m)
    levels = n.bit_length() - 1
    for _ in range(levels - 2):
        r = _mm3_rows(jnp.concatenate([p, x], axis=0), p)
        x = x + r[n:]
        p = r[:n]
    return x + _mm3_rows(x, p)


def _gdn_chunk_kernel(q_ref, k_ref, v_ref, gb_ref, o_ref, s_ref):
    lb = q_ref.shape[1]
    c = CHUNK
    dk = HEAD_DIM_GDN

    @pl.when(pl.program_id(1) == 0)
    def _():
        s_ref[...] = jnp.zeros(s_ref.shape, F32)

    row = lax.broadcasted_iota(jnp.int32, (c, c), 0)
    col = lax.broadcasted_iota(jnp.int32, (c, c), 1)
    incl = row >= col
    strict = row > col
    eye = (row == col).astype(F32)
    tri = incl.astype(BF16)
    lane = lax.broadcasted_iota(jnp.int32, (c, LANES), 1)

    def body(ci, carry):
        r0 = pl.multiple_of(ci * c, c)
        gb = gb_ref[0, pl.ds(r0, c), :]
        g = _cumsum_rows(tri, gb)
        g_t = g.T
        g_last = g[c - 1:c, :]
        eg = jnp.exp(g)
        egl = jnp.exp(g_last - g)
        eg_last = jnp.exp(g_last)
        for hh in range(N_HEADS_GDN):
            sl = slice(hh * dk, (hh + 1) * dk)
            q = q_ref[0, pl.ds(r0, c), sl].astype(F32)
            k = k_ref[0, pl.ds(r0, c), sl].astype(F32)
            v = v_ref[0, pl.ds(r0, c), sl].astype(F32)
            beta = _lane_col(gb, N_HEADS_GDN + hh, lane)
            g_col = _lane_col(g, hh, lane)
            eg_col = _lane_col(eg, hh, lane)
            egl_col = _lane_col(egl, hh, lane)
            egl_last = _lane_col(eg_last, hh, lane[0:1])
            diff = g_col - g_t[hh:hh + 1, :]
            decay = jnp.where(incl, jnp.exp(jnp.where(incl, diff, 0.0)), 0.0)
            kb = k * beta
            kq = _dot_nt(jnp.concatenate([kb, q], axis=0).astype(BF16), k.astype(BF16))
            a_strict = jnp.where(strict, kq[:c] * decay, 0.0)
            qk = kq[c:] * decay
            t_inv = _unit_lower_inverse(a_strict, eye)
            rhs = jnp.concatenate([v * beta, kb * eg_col], axis=1)
            uw = _dot(t_inv.astype(BF16), rhs.astype(BF16))
            u = uw[:, :dk]
            w = uw[:, dk:]
            s = s_ref[hh]
            ws_qs = _dot(jnp.concatenate([w, q * eg_col], axis=0).astype(BF16), s.astype(BF16))
            v_new = u - ws_qs[:c]
            o = ws_qs[c:] + _dot(qk.astype(BF16), v_new.astype(BF16))
            kd_t = (k * egl_col).T
            s_ref[hh] = s * egl_last + _dot(kd_t.astype(BF16), v_new.astype(BF16))
            o_ref[0, pl.ds(r0, c), sl] = o.astype(o_ref.dtype)
        return carry

    lax.fori_loop(0, lb // c, body, 0)


def _gdn_chunk(q, k, v, gb):
    b, l, d = q.shape
    lb = min(GDN_BLOCK, l)
    row = lambda i, j: (i, j, 0)
    return pl.pallas_call(
        _gdn_chunk_kernel,
        grid=(b, l // lb),
        in_specs=[pl.BlockSpec((1, lb, d), row)] * 3 + [pl.BlockSpec((1, lb, LANES), row)],
        out_specs=pl.BlockSpec((1, lb, d), row),
        out_shape=jax.ShapeDtypeStruct((b, l, d), F32),
        scratch_shapes=[pltpu.VMEM((N_HEADS_GDN, HEAD_DIM_GDN, HEAD_DIM_GDN), F32)],
        compiler_params=_params(("arbitrary", "arbitrary")),
        name="gdn_chunk",
    )(q, k, v, gb)


def _fox_inproj_kernel(x_ref, sc_ref, sh_ref, w_ref, wf_ref, bf_ref,
                       q_ref, k_ref, v_ref, g_ref, cum_t_ref, carry_ref):
    tm = x_ref.shape[1]
    d = x_ref.shape[2]
    h = _rms_mod(x_ref[0], sc_ref[0], sh_ref[0])
    hb = h.astype(BF16)
    for s, o_ref in enumerate((q_ref, k_ref, v_ref)):
        o_ref[0] = _dot(hb, w_ref[:, s * d:(s + 1) * d]).astype(o_ref.dtype)
    g_ref[0] = _sigmoid(_dot(hb, w_ref[:, 3 * d:4 * d])).astype(g_ref.dtype)

    @pl.when(pl.program_id(1) == 0)
    def _():
        carry_ref[...] = jnp.zeros(carry_ref.shape, F32)

    f_logit = _dot_split_weights(h, wf_ref[...], LANES) + bf_ref[...]
    log_f = -_softplus(-f_logit)
    row = lax.broadcasted_iota(jnp.int32, (tm, tm), 0)
    col = lax.broadcasted_iota(jnp.int32, (tm, tm), 1)
    cum = _cumsum_rows((row >= col).astype(BF16), log_f) + carry_ref[...]
    carry_ref[...] = cum[tm - 1:tm, :]
    cum_t_ref[0] = cum.T


def _fox_inproj(x, sc, sh, w_main, w_f, b_f):
    b, l, d = x.shape
    tm = min(ROW_TILE, l)
    act = jax.ShapeDtypeStruct((b, l, d), BF16)
    row = lambda i, j: (i, j, 0)
    vec = lambda i, j: (i, 0, 0)
    const = lambda i, j: (0, 0)
    return pl.pallas_call(
        _fox_inproj_kernel,
        grid=(b, l // tm),
        in_specs=[
            pl.BlockSpec((1, tm, d), row),
            pl.BlockSpec((1, 1, d), vec),
            pl.BlockSpec((1, 1, d), vec),
            pl.BlockSpec(w_main.shape, const),
            pl.BlockSpec(w_f.shape, const),
            pl.BlockSpec(b_f.shape, const),
        ],
        out_specs=[pl.BlockSpec((1, tm, d), row)] * 4 + [pl.BlockSpec((1, LANES, tm), lambda i, j: (i, 0, j))],
        out_shape=[act, act, act, act, jax.ShapeDtypeStruct((b, LANES, l), F32)],
        scratch_shapes=[pltpu.VMEM((1, LANES), F32)],
        compiler_params=_params(("arbitrary", "arbitrary")),
        name="fox_inproj",
    )(x, sc, sh, w_main, w_f, b_f)


def _fox_attn_kernel(q_ref, k_ref, v_ref, cum_ref, qg_ref, kg_ref, o_ref, kn_ref, m_ref, l_ref, acc_ref):
    tq = q_ref.shape[1]
    hd = HEAD_DIM_FOX
    qi = pl.program_id(2)
    lane = lax.broadcasted_iota(jnp.int32, (1, LANES), 1)
    first = lane < hd

    def head_norm(t, gain):
        sq = t * t
        s0 = jnp.sum(jnp.where(first, sq, 0.0), axis=1, keepdims=True)
        s1 = jnp.sum(jnp.where(first, 0.0, sq), axis=1, keepdims=True)
        ms = jnp.where(first, s0, s1) * (1.0 / hd)
        return t * lax.rsqrt(ms + EPS) * gain

    @pl.when(qi == 0)
    def _():
        kn_ref[...] = head_norm(k_ref[0].astype(F32), kg_ref[...]).astype(kn_ref.dtype)

    qn = head_norm(q_ref[0].astype(F32), qg_ref[...]) * (hd ** -0.5)
    q2 = jnp.concatenate([jnp.where(first, qn, 0.0), jnp.where(first, 0.0, qn)], axis=0).astype(BF16)

    m_ref[...] = jnp.full(m_ref.shape, NEG_BIG, F32)
    l_ref[...] = jnp.zeros(l_ref.shape, F32)
    acc_ref[...] = jnp.zeros(acc_ref.shape, F32)

    def step(j, masked):
        k0 = pl.multiple_of(j * tq, tq)
        kb = kn_ref[pl.ds(k0, tq), :]
        vb = v_ref[0, pl.ds(k0, tq), :]
        ck = cum_ref[0, 0, :, pl.ds(k0, tq)]
        s = _dot_nt(q2, kb)
        bias = jnp.concatenate([jnp.broadcast_to(ck[0:1], (tq, tq)),
                                jnp.broadcast_to(ck[1:2], (tq, tq))], axis=0)
        s = s - bias
        if masked:
            r = lax.broadcasted_iota(jnp.int32, (tq, tq), 0)
            c = lax.broadcasted_iota(jnp.int32, (tq, tq), 1)
            keep = jnp.concatenate([r >= c, r >= c], axis=0)
            s = jnp.where(keep, s, NEG_BIG)
        m_old = m_ref[...]
        m_new = jnp.maximum(m_old, jnp.max(s, axis=1, keepdims=True))
        alpha = jnp.exp(m_old - m_new)
        p = jnp.exp(s - m_new)
        l_ref[...] = alpha * l_ref[...] + jnp.sum(p, axis=1, keepdims=True)
        acc_ref[...] = alpha * acc_ref[...] + _dot(p.astype(BF16), vb)
        m_ref[...] = m_new

    def full_step(j, carry):
        step(j, False)
        return carry

    lax.fori_loop(0, qi, full_step, 0)
    step(qi, True)

    out = acc_ref[...] / l_ref[...]
    o_ref[0] = jnp.where(first, out[:tq], out[tq:]).astype(o_ref.dtype)


def _fox_attn(q, k, v, cum, q_gain, k_gain):
    b, l, d = q.shape
    tq = min(ATTN_TILE, l)
    pairs = d // LANES
    return pl.pallas_call(
        _fox_attn_kernel,
        grid=(b, pairs, l // tq),
        in_specs=[
            pl.BlockSpec((1, tq, LANES), lambda i, p, j: (i, j, p)),
            pl.BlockSpec((1, l, LANES), lambda i, p, j: (i, 0, p)),
            pl.BlockSpec((1, l, LANES), lambda i, p, j: (i, 0, p)),
            pl.BlockSpec((1, 1, 2, l), lambda i, p, j: (i, p, 0, 0)),
            pl.BlockSpec((1, LANES), lambda i, p, j: (0, 0)),
            pl.BlockSpec((1, LANES), lambda i, p, j: (0, 0)),
        ],
        out_specs=pl.BlockSpec((1, tq, LANES), lambda i, p, j: (i, j, p)),
        out_shape=jax.ShapeDtypeStruct((b, l, d), BF16),
        scratch_shapes=[
            pltpu.VMEM((l, LANES), BF16),
            pltpu.VMEM((2 * tq, 1), F32),
            pltpu.VMEM((2 * tq, 1), F32),
            pltpu.VMEM((2 * tq, LANES), F32),
        ],
        compiler_params=_params(("arbitrary", "arbitrary", "arbitrary")),
        name="fox_attn",
    )(q, k, v, cum, q_gain, k_gain)


def _outproj_kernel(head_norm, o_ref, gate_ref, gain_ref, w_ref, x_ref, g1_ref, sc_ref, sh_ref,
                    wr_ref, br_ref, x1_ref, h2_ref, lg_ref):
    o = o_ref[0].astype(F32)
    gate = gate_ref[0].astype(F32)
    if head_norm:
        parts = []
        for hh in range(N_HEADS_GDN):
            seg = o[:, hh * HEAD_DIM_GDN:(hh + 1) * HEAD_DIM_GDN]
            parts.append(seg * lax.rsqrt(jnp.mean(seg * seg, axis=-1, keepdims=True) + EPS))
        o = jnp.concatenate(parts, axis=1) * gain_ref[...]
    y = _dot((o * gate).astype(BF16), w_ref[...])
    x1 = x_ref[0] + g1_ref[0] * y
    x1_ref[0] = x1
    h2 = _rms_mod(x1, sc_ref[0], sh_ref[0])
    h2_ref[0] = h2
    lg_ref[0] = _dot_split_weights(h2, wr_ref[...], LANES) + br_ref[...]


def _outproj(head_norm, o, gate, gain, w_out, x, g1, sc2, sh2, w_route, b_route):
    b, l, d = x.shape
    tm = min(ROW_TILE, l)
    row = lambda i, j: (i, j, 0)
    vec = lambda i, j: (i, 0, 0)
    const = lambda i, j: (0, 0)
    return pl.pallas_call(
        functools.partial(_outproj_kernel, head_norm),
        grid=(b, l // tm),
        in_specs=[
            pl.BlockSpec((1, tm, d), row),
            pl.BlockSpec((1, tm, d), row),
            pl.BlockSpec(gain.shape, const),
            pl.BlockSpec(w_out.shape, const),
            pl.BlockSpec((1, tm, d), row),
            pl.BlockSpec((1, 1, d), vec),
            pl.BlockSpec((1, 1, d), vec),
            pl.BlockSpec((1, 1, d), vec),
            pl.BlockSpec(w_route.shape, const),
            pl.BlockSpec(b_route.shape, const),
        ],
        out_specs=[pl.BlockSpec((1, tm, d), row), pl.BlockSpec((1, tm, d), row),
                   pl.BlockSpec((1, tm, LANES), row)],
        out_shape=[jax.ShapeDtypeStruct((b, l, d), F32), jax.ShapeDtypeStruct((b, l, d), F32),
                   jax.ShapeDtypeStruct((b, l, LANES), F32)],
        compiler_params=_params(("arbitrary", "arbitrary")),
        name="outproj",
    )(o, gate, gain, w_out, x, g1, sc2, sh2, w_route, b_route)


def _route_kernel(lg_ref, eid_ref, wt_ref):
    lg = lg_ref[...]
    lane = lax.broadcasted_iota(jnp.int32, lg.shape, 1)
    big = jnp.int32(1 << 20)

    def first_argmax(v, vmax):
        return jnp.min(jnp.where(v == vmax, lane, big), axis=1, keepdims=True)

    is_group = lane < N_GROUPS
    gl = jnp.where(is_group, lg, NEG_BIG)
    g_max = jnp.max(gl, axis=1, keepdims=True)
    g_sum = jnp.sum(jnp.where(is_group, jnp.exp(gl - g_max), 0.0), axis=1, keepdims=True)
    g_p = 1.0 / g_sum
    g_idx = first_argmax(gl, g_max)
    lo = N_GROUPS + EXPERTS_PER_GROUP * g_idx
    el = jnp.where((lane >= lo) & (lane < lo + EXPERTS_PER_GROUP), lg, NEG_BIG)
    m1 = jnp.max(el, axis=1, keepdims=True)
    i1 = first_argmax(el, m1)
    el2 = jnp.where(lane == i1, NEG_BIG, el)
    m2 = jnp.max(el2, axis=1, keepdims=True)
    i2 = first_argmax(el2, m2)
    e = jnp.exp(m2 - m1)
    w1 = g_p / (1.0 + e)
    w2 = w1 * e
    eid_ref[...] = jnp.where(lane == 0, i1 - N_GROUPS, i2 - N_GROUPS)
    wt_ref[...] = jnp.where(lane == 0, w1, w2)


def _route(logits):
    t = logits.shape[0]
    tm = min(ROW_TILE, t)
    row = lambda i: (i, 0)
    return pl.pallas_call(
        _route_kernel,
        grid=(t // tm,),
        in_specs=[pl.BlockSpec((tm, LANES), row)],
        out_specs=[pl.BlockSpec((tm, LANES), row)] * 2,
        out_shape=[jax.ShapeDtypeStruct((t, LANES), jnp.int32), jax.ShapeDtypeStruct((t, LANES), F32)],
        compiler_params=_params(("arbitrary",)),
        name="route",
    )(logits)


def _row_copy(src_ref, src_row, dst_ref, dst_row, sem):
    return pltpu.make_async_copy(src_ref.at[pl.ds(src_row, 1)], dst_ref.at[pl.ds(dst_row, 1)], sem)


def _dispatch_kernel(pos_ref, h_ref, init_ref, xs_ref, sem):
    del init_ref
    tt = h_ref.shape[0]

    def issue(r, carry):
        for j in range(TOP_K):
            _row_copy(h_ref, r, xs_ref, pos_ref[0, 0, TOP_K * r + j], sem).start()
        return carry

    lax.fori_loop(0, tt, issue, 0)

    def drain(r, carry):
        for j in range(TOP_K):
            _row_copy(h_ref, 0, xs_ref, 0, sem).wait()
        return carry

    lax.fori_loop(0, tt, drain, 0)


def _dispatch(pos, h2, n_rows):
    t, d = h2.shape
    tt = min(TOKEN_TILE, t)
    nb = t // tt
    return pl.pallas_call(
        _dispatch_kernel,
        grid=(nb,),
        in_specs=[
            pl.BlockSpec((1, 1, TOP_K * tt), lambda i: (i, 0, 0), memory_space=pltpu.SMEM),
            pl.BlockSpec((tt, d), lambda i: (i, 0)),
            pl.BlockSpec(memory_space=pl.ANY),
        ],
        out_specs=pl.BlockSpec(memory_space=pl.ANY),
        out_shape=jax.ShapeDtypeStruct((n_rows, d), F32),
        scratch_shapes=[pltpu.SemaphoreType.DMA(())],
        input_output_aliases={2: 0},
        compiler_params=_params(("arbitrary",)),
        name="moe_dispatch",
    )(pos.reshape(nb, 1, TOP_K * tt), h2, jnp.zeros((n_rows, d), F32))


def _gmm_kernel(te_ref, tf_ref, xs_ref, wgu_ref, wd_ref, o_ref, wgu_bf, wd_bf):
    i = pl.program_id(0)

    @pl.when(tf_ref[i] == 1)
    def _():
        wgu_bf[...] = wgu_ref[0].astype(BF16)
        wd_bf[...] = wd_ref[0].astype(BF16)

    f = wd_bf.shape[0]
    hu = _dot(xs_ref[...].astype(BF16), wgu_bf[...])
    act = _silu(hu[:, :f]) * hu[:, f:]
    o_ref[...] = _dot(act.astype(BF16), wd_bf[...])


def _gmm(tile_expert, tile_first, xs, w_gate_up, w_down):
    n_rows, d = xs.shape
    tm = MOE_TILE
    f2 = w_gate_up.shape[2]
    grid_spec = pltpu.PrefetchScalarGridSpec(
        num_scalar_prefetch=2,
        grid=(n_rows // tm,),
        in_specs=[
            pl.BlockSpec((tm, d), lambda i, te, tf: (i, 0)),
            pl.BlockSpec((1, d, f2), lambda i, te, tf: (te[i], 0, 0)),
            pl.BlockSpec((1, f2 // 2, d), lambda i, te, tf: (te[i], 0, 0)),
        ],
        out_specs=pl.BlockSpec((tm, d), lambda i, te, tf: (i, 0)),
        scratch_shapes=[pltpu.VMEM((d, f2), BF16), pltpu.VMEM((f2 // 2, d), BF16)],
    )
    return pl.pallas_call(
        _gmm_kernel,
        grid_spec=grid_spec,
        out_shape=jax.ShapeDtypeStruct((n_rows, d), F32),
        compiler_params=_params(("arbitrary",)),
        name="moe_gmm",
    )(tile_expert, tile_first, xs, w_gate_up, w_down)


def _combine_kernel(final, pos_ref, ys_ref, wt_ref, x_ref, g2_ref, fg_ref, o_ref, buf_ref, sem):
    tt = x_ref.shape[0]

    def issue(r, carry):
        for j in range(TOP_K):
            _row_copy(ys_ref, pos_ref[0, 0, TOP_K * r + j], buf_ref.at[j], r, sem).start()
        return carry

    lax.fori_loop(0, tt, issue, 0)

    def drain(r, carry):
        for j in range(TOP_K):
            _row_copy(ys_ref, 0, buf_ref.at[j], 0, sem).wait()
        return carry

    lax.fori_loop(0, tt, drain, 0)

    wt = wt_ref[...]
    lane = lax.broadcasted_iota(jnp.int32, wt.shape, 1)
    y = _lane_col(wt, 0, lane) * buf_ref[0] + _lane_col(wt, 1, lane) * buf_ref[1]
    x2 = x_ref[...] + g2_ref[0] * y
    if final:
        x2 = x2 * lax.rsqrt(jnp.mean(x2 * x2, axis=-1, keepdims=True) + EPS) * fg_ref[...]
    o_ref[...] = x2


def _combine(final, pos, ys, wt, x1, g2, final_gain, tokens_per_seq):
    t, d = x1.shape
    tt = min(TOKEN_TILE, tokens_per_seq)
    nb = t // tt
    per_seq = tokens_per_seq // tt
    return pl.pallas_call(
        functools.partial(_combine_kernel, final),
        grid=(nb,),
        in_specs=[
            pl.BlockSpec((1, 1, TOP_K * tt), lambda i: (i, 0, 0), memory_space=pltpu.SMEM),
            pl.BlockSpec(memory_space=pl.ANY),
            pl.BlockSpec((tt, LANES), lambda i: (i, 0)),
            pl.BlockSpec((tt, d), lambda i: (i, 0)),
            pl.BlockSpec((1, 1, d), lambda i: (i // per_seq, 0, 0)),
            pl.BlockSpec((1, d), lambda i: (0, 0)),
        ],
        out_specs=pl.BlockSpec((tt, d), lambda i: (i, 0)),
        out_shape=jax.ShapeDtypeStruct((t, d), F32),
        scratch_shapes=[pltpu.VMEM((TOP_K, tt, d), F32), pltpu.SemaphoreType.DMA(())],
        compiler_params=_params(("arbitrary",)),
        name="moe_combine",
    )(pos.reshape(nb, 1, TOP_K * tt), ys, wt, x1, g2, final_gain)


def _sorted_positions(eid, tile):
    flat = eid.reshape(-1)
    n_tiles = flat.shape[0] // tile + N_EXPERTS
    onehot = (flat[:, None] == jnp.arange(N_EXPERTS, dtype=jnp.int32)[None, :]).astype(jnp.int32)
    csum = jnp.cumsum(onehot, axis=0)
    rank = jnp.sum(onehot * csum, axis=1) - 1
    counts = csum[-1]
    tiles_per = (counts + tile - 1) // tile
    tile_end = jnp.cumsum(tiles_per)
    offs = (tile_end - tiles_per) * tile
    pos = jnp.sum(onehot * offs[None, :], axis=1) + rank
    tile_ids = jnp.arange(n_tiles, dtype=jnp.int32)
    tile_expert = jnp.minimum(jnp.sum((tile_ids[:, None] >= tile_end[None, :]).astype(jnp.int32), axis=1),
                              N_EXPERTS - 1)
    prev = jnp.concatenate([jnp.full((1,), -1, jnp.int32), tile_expert[:-1]])
    tile_first = (tile_expert != prev).astype(jnp.int32)
    return pos.astype(jnp.int32), tile_expert.astype(jnp.int32), tile_first, n_tiles * tile


def _hier_moe(final, layer, h2, logits, x1, g2, w_gate_up, w_down, final_gain, tokens_per_seq):
    eid_full, wt = _route(logits)
    pos, tile_expert, tile_first, n_rows = _sorted_positions(eid_full[:, :TOP_K], MOE_TILE)
    xs = _dispatch(pos, h2, n_rows)
    ys = _gmm(tile_expert + layer * N_EXPERTS, tile_first, xs, w_gate_up, w_down)
    return _combine(final, pos, ys, wt, x1, g2, final_gain, tokens_per_seq)


def _hi_lo_cols(w, n):
    k, m = w.shape
    wp = jnp.zeros((k, n), F32).at[:, :m].set(w)
    hi = wp.astype(BF16)
    lo = (wp - hi.astype(F32)).astype(BF16)
    return jnp.concatenate([hi, lo], axis=1)


def _pad_row(v, n):
    return jnp.zeros((1, n), F32).at[0, :v.shape[0]].set(v)


def kernel(x, c, ada_w, ada_b, gdn_w_in, gdn_conv, gdn_a_log, gdn_dt_bias, gdn_norm, gdn_w_out,
           fox_w_in, fox_b_f, fox_q_norm, fox_k_norm, fox_w_out,
           moe_w_group, moe_b_group, moe_w_router, moe_b_router, moe_w_gate_up, moe_w_down,
           final_norm):
    b, l, d = x.shape
    depth = ada_w.shape[0]
    t = b * l
    ada = _ada(c, ada_w, ada_b)
    w_gate_up = moe_w_gate_up.reshape(depth * N_EXPERTS, d, moe_w_gate_up.shape[-1])
    w_down = moe_w_down.reshape(depth * N_EXPERTS, moe_w_down.shape[-2], d)
    final_gain = final_norm.reshape(1, d)

    for i in range(depth):
        sh1, sc1, g1, sh2, sc2, g2 = [ada[i, :, s * d:(s + 1) * d].reshape(b, 1, d) for s in range(6)]
        j = i // 2
        w_route = _hi_lo_cols(jnp.concatenate([moe_w_group[i], moe_w_router[i]], axis=1), LANES)
        b_route = _pad_row(jnp.concatenate([moe_b_group[i], moe_b_router[i]]), LANES)
        if i % 2 == 0:
            w_in = gdn_w_in[j]
            q, k, v, gate, gb = _gdn_inproj(
                x, sc1, sh1, w_in[:, :4 * d].astype(BF16), _hi_lo_cols(w_in[:, 4 * d:], LANES),
                gdn_conv[j], jnp.concatenate([_pad_row(gdn_a_log[j], LANES), _pad_row(gdn_dt_bias[j], LANES)]))
            o = _gdn_chunk(q, k, v, gb)
            gain = jnp.tile(gdn_norm[j], N_HEADS_GDN).reshape(1, d)
            x1, h2, logits = _outproj(True, o, gate, gain, gdn_w_out[j].astype(BF16), x, g1, sc2, sh2,
                                      w_route, b_route)
        else:
            w_in = fox_w_in[j]
            q, k, v, gate, cum_t = _fox_inproj(
                x, sc1, sh1, w_in[:, :4 * d].astype(BF16), _hi_lo_cols(w_in[:, 4 * d:], LANES),
                _pad_row(fox_b_f[j], LANES))
            cum = cum_t[:, :N_HEADS_FOX, :].reshape(b, N_HEADS_FOX // 2, 2, l)
            q_gain = jnp.tile(fox_q_norm[j], 2).reshape(1, LANES)
            k_gain = jnp.tile(fox_k_norm[j], 2).reshape(1, LANES)
            o = _fox_attn(q, k, v, cum, q_gain, k_gain)
            x1, h2, logits = _outproj(False, o, gate, final_gain, fox_w_out[j].astype(BF16), x, g1, sc2, sh2,
                                      w_route, b_route)
        x = _hier_moe(i == depth - 1, i, h2.reshape(t, d), logits.reshape(t, LANES), x1.reshape(t, d), g2,
                      w_gate_up, w_down, final_gain, l).reshape(b, l, d)
    return x
```

```python
import functools

import jax
import jax.numpy as jnp
from jax import lax
from jax.experimental import pallas as pl
from jax.experimental.pallas import tpu as pltpu
from jax.experimental.pallas import tpu_sc as plsc

F32 = jnp.float32
BF16 = jnp.bfloat16

EPS = 1e-6
CHUNK = 64
N_HEADS_GDN = 8
HEAD_DIM_GDN = 128
N_HEADS_FOX = 16
HEAD_DIM_FOX = 64
CONV_K = 4
N_GROUPS = 4
EXPERTS_PER_GROUP = 8
N_EXPERTS = N_GROUPS * EXPERTS_PER_GROUP
TOP_K = 2

LANES = 128
SUBLANES = 8
VMEM_LIMIT_BYTES = 56 * 1024 * 1024

ROW_TILE = 512
ATTN_TILE = 512
ATTN_UNROLL = 2
GDN_BLOCK = 512
GDN_CHUNKS_PER_ITER = 4
GDN_HEADS_PACKED = 4
MOE_TILE = 256
GMM_SLOTS = 4
TOKEN_TILE = 512
SC_WINDOW = 64
NEG_BIG = -1e30
LOG2E = 1.4426950408889634


def _params(semantics):
    return pltpu.CompilerParams(dimension_semantics=semantics, vmem_limit_bytes=VMEM_LIMIT_BYTES)


def _sigmoid(x):
    return 0.5 * jnp.tanh(0.5 * x) + 0.5


def _silu(x):
    return x * _sigmoid(x)


def _softplus(x):
    return jnp.maximum(x, 0.0) + jnp.log(1.0 + jnp.exp(-jnp.abs(x)))


def _split2(a):
    hi = a.astype(BF16)
    lo = (a - hi.astype(F32)).astype(BF16)
    return hi, lo


def _split3(a):
    hi = a.astype(BF16)
    r = a - hi.astype(F32)
    mid = r.astype(BF16)
    lo = (r - mid.astype(F32)).astype(BF16)
    return hi, mid, lo


def _dot(a, b):
    return jnp.dot(a, b, preferred_element_type=F32)


def _dot_nt(a, b):
    return lax.dot_general(a, b, (((1,), (1,)), ((), ())), preferred_element_type=F32)


def _dot_split_weights(a, w_hi_lo, n):
    a_hi, a_lo = _split2(a)
    r = _dot(a_hi, w_hi_lo)
    return r[:, :n] + r[:, n:] + _dot(a_lo, w_hi_lo[:, :n])


def _cumsum_rows(tri_bf16, v):
    hi, mid, lo = _split3(v)
    r = _dot(tri_bf16, jnp.concatenate([hi, mid], axis=1))
    return r[:, :LANES] + r[:, LANES:] + _dot(tri_bf16, lo)


def _rms_mod(x, sc, sh):
    ms = jnp.mean(x * x, axis=-1, keepdims=True)
    return x * lax.rsqrt(ms + EPS) * (1.0 + sc) + sh


def _pack_bf16_pairs(a):
    n = a.shape[1] // 2
    bits = lax.bitcast_convert_type(a.astype(BF16).astype(F32), jnp.uint32)
    return (bits[:, :n] & jnp.uint32(0xFFFF0000)) | (bits[:, n:] >> 16)


def _unpack_bf16_pairs(p):
    hi = lax.bitcast_convert_type(p & jnp.uint32(0xFFFF0000), F32)
    lo = lax.bitcast_convert_type(p << 16, F32)
    return hi, lo


def _lane_col(v, idx, lane):
    return jnp.sum(jnp.where(lane == idx, v, 0.0), axis=1, keepdims=True)


def _ada_kernel(c_ref, w_ref, b_ref, o_ref):
    c = c_ref[...]
    o_ref[0] = _dot(_silu(c).astype(BF16), w_ref[0].astype(BF16)) + b_ref[0]


def _ada(c, ada_w, ada_b):
    depth, d, n = ada_w.shape
    b = c.shape[0]
    tn = 1536
    return pl.pallas_call(
        _ada_kernel,
        grid=(depth, n // tn),
        in_specs=[
            pl.BlockSpec((b, d), lambda i, j: (0, 0)),
            pl.BlockSpec((1, d, tn), lambda i, j: (i, 0, j)),
            pl.BlockSpec((1, 1, tn), lambda i, j: (i, 0, j)),
        ],
        out_specs=pl.BlockSpec((1, b, tn), lambda i, j: (i, 0, j)),
        out_shape=jax.ShapeDtypeStruct((depth, b, n), F32),
        compiler_params=_params(("arbitrary", "arbitrary")),
        name="ada",
    )(c, ada_w, ada_b.reshape(depth, 1, n))


def _gdn_inproj_kernel(x_ref, sc_ref, sh_ref, w_ref, wab_ref, conv_ref, prm_ref,
                       q_ref, k_ref, v_ref, gate_ref, gb_ref, halo_ref):
    tm = x_ref.shape[1]
    d = x_ref.shape[2]
    h = _rms_mod(x_ref[0], sc_ref[0], sh_ref[0])
    hb = h.astype(BF16)

    @pl.when(pl.program_id(1) == 0)
    def _():
        halo_ref[:, 0:SUBLANES, :] = jnp.zeros((3, SUBLANES, d), F32)

    for s, o_ref in enumerate((q_ref, k_ref, v_ref)):
        raw = _dot(hb, w_ref[:, s * d:(s + 1) * d])
        halo_ref[s, SUBLANES:SUBLANES + tm, :] = raw
        cw = conv_ref[:, s * d:(s + 1) * d]
        y = raw * cw[3:4]
        for j in range(CONV_K - 1):
            off = SUBLANES - (CONV_K - 1) + j
            y = y + halo_ref[s, off:off + tm, :] * cw[j:j + 1]
        halo_ref[s, 0:SUBLANES, :] = halo_ref[s, tm:tm + SUBLANES, :]
        y = _silu(y)
        if s < 2:
            scale = HEAD_DIM_GDN ** -0.5 if s == 0 else 1.0
            for hh in range(N_HEADS_GDN):
                seg = y[:, hh * HEAD_DIM_GDN:(hh + 1) * HEAD_DIM_GDN]
                inv = lax.rsqrt(jnp.sum(seg * seg, axis=-1, keepdims=True) + EPS) * scale
                o_ref[0, :, hh * HEAD_DIM_GDN:(hh + 1) * HEAD_DIM_GDN] = (seg * inv).astype(o_ref.dtype)
        else:
            o_ref[0] = y.astype(o_ref.dtype)

    gate_ref[0] = _silu(_dot(hb, w_ref[:, 3 * d:4 * d])).astype(gate_ref.dtype)

    ab = _dot_split_weights(h, wab_ref[...], LANES)
    log_alpha = -jnp.exp(prm_ref[0:1, :]) * _softplus(ab + prm_ref[1:2, :])
    lane = lax.broadcasted_iota(jnp.int32, ab.shape, 1)
    gb_ref[0] = jnp.where(lane < N_HEADS_GDN, log_alpha, _sigmoid(ab))


def _gdn_inproj(x, sc, sh, w_main, w_ab, conv_w, prm):
    b, l, d = x.shape
    tm = min(ROW_TILE, l)
    act = jax.ShapeDtypeStruct((b, l, d), BF16)
    row = lambda i, j: (i, j, 0)
    vec = lambda i, j: (i, 0, 0)
    const = lambda i, j: (0, 0)
    return pl.pallas_call(
        _gdn_inproj_kernel,
        grid=(b, l // tm),
        in_specs=[
            pl.BlockSpec((1, tm, d), row),
            pl.BlockSpec((1, 1, d), vec),
            pl.BlockSpec((1, 1, d), vec),
            pl.BlockSpec(w_main.shape, const),
            pl.BlockSpec(w_ab.shape, const),
            pl.BlockSpec(conv_w.shape, const),
            pl.BlockSpec(prm.shape, const),
        ],
        out_specs=[pl.BlockSpec((1, tm, d), row)] * 4 + [pl.BlockSpec((1, tm, LANES), row)],
        out_shape=[act, act, act, act, jax.ShapeDtypeStruct((b, l, LANES), F32)],
        scratch_shapes=[pltpu.VMEM((3, tm + SUBLANES, d), F32)],
        compiler_params=_params(("arbitrary", "arbitrary")),
        name="gdn_inproj",
    )(x, sc, sh, w_main, w_ab, conv_w, prm)


def _block_diag(r, blk, n_blk):
    return jnp.concatenate([jnp.where(blk == b, r, jnp.zeros_like(r)) for b in range(n_blk)], axis=0)


def _mm_packed(lhs, rhs, blk, n_blk):
    m = lhs.shape[0]
    l_hi, l_lo = _split2(lhs)
    r_hi, r_lo = _split2(rhs)
    t = _dot(jnp.concatenate([l_hi, l_lo], axis=0), _block_diag(r_hi, blk, n_blk))
    return t[:m] + t[m:] + _dot(l_hi, _block_diag(r_lo, blk, n_blk))


def _unit_lower_inverse(a_strict, eye, blk, n_blk):
    n = a_strict[0].shape[0]
    levels = n.bit_length() - 1
    m = [-a for a in a_strict]
    x = [eye + mi for mi in m]
    p = [_mm_packed(mi, mi, blk, n_blk) for mi in m]
    for _ in range(levels - 2):
        r = [_mm_packed(jnp.concatenate([pi, xi], axis=0), pi, blk, n_blk) for pi, xi in zip(p, x)]
        x = [xi + ri[n:] for xi, ri in zip(x, r)]
        p = [ri[:n] for ri in r]
    return [xi + _mm_packed(xi, pi, blk, n_blk) for xi, pi in zip(x, p)]


def _gdn_chunk_kernel(q_ref, k_ref, v_ref, gb_ref, o_ref, s_ref):
    lb = q_ref.shape[1]
    c = CHUNK
    dk = HEAD_DIM_GDN
    nc = GDN_CHUNKS_PER_ITER
    heads = range(N_HEADS_GDN)
    sl = [slice(hh * dk, (hh + 1) * dk) for hh in heads]

    @pl.when(pl.program_id(1) == 0)
    def _():
        s_ref[...] = jnp.zeros(s_ref.shape, F32)

    pk = GDN_HEADS_PACKED
    groups = N_HEADS_GDN // pk
    tri = (lax.broadcasted_iota(jnp.int32, (c, c), 0) >= lax.broadcasted_iota(jnp.int32, (c, c), 1)).astype(BF16)
    lane = lax.broadcasted_iota(jnp.int32, (c, LANES), 1)
    side = lane // c
    row_p = lax.broadcasted_iota(jnp.int32, (c, pk * c), 0)
    lane_p = lax.broadcasted_iota(jnp.int32, (c, pk * c), 1)
    blk = lane_p // c
    col_p = lane_p - blk * c
    incl = row_p >= col_p
    strict = row_p > col_p
    diag = row_p == col_p
    eye = diag.astype(F32)
    blk_k = lax.broadcasted_iota(jnp.int32, (c, pk * dk), 1) // dk

    def per_head(cols, width):
        return jnp.concatenate([jnp.broadcast_to(col, (c, width)) for col in cols], axis=1)

    def body(it, carry):
        r0, q, kd_t, egl_last, rhs_pair, kq, decay, eg_cols = [], [], [], [], [], [], [], []
        for j in range(nc):
            r = pl.multiple_of((it * nc + j) * c, c)
            r0.append(r)
            gb = gb_ref[0, pl.ds(r, c), :]
            g = _cumsum_rows(tri, gb)
            g_last = g[c - 1:c, :]
            eg = jnp.exp(g)
            egl = jnp.exp(g_last - g)
            eg_last = jnp.exp(g_last)
            for gi in range(groups):
                hs = [gi * pk + b for b in range(pk)]
                cols = slice(gi * pk * dk, (gi + 1) * pk * dk)
                q4 = q_ref[0, pl.ds(r, c), cols].astype(F32)
                k4 = k_ref[0, pl.ds(r, c), cols].astype(F32)
                v4 = v_ref[0, pl.ds(r, c), cols].astype(F32)
                beta4 = per_head([_lane_col(gb, N_HEADS_GDN + hh, lane) for hh in hs], dk)
                eg4 = per_head([_lane_col(eg, hh, lane) for hh in hs], dk)
                egl4 = per_head([_lane_col(egl, hh, lane) for hh in hs], dk)
                kb4 = k4 * beta4
                k4b = k4.astype(BF16)
                k_diag = jnp.concatenate([jnp.where(blk_k == b, k4b, jnp.zeros_like(k4b)) for b in range(pk)],
                                         axis=0)
                kq.append(_dot_nt(jnp.concatenate([kb4, q4], axis=0).astype(BF16), k_diag))
                g_col = per_head([_lane_col(g, hh, lane) for hh in hs], c)
                g_row = jnp.sum(jnp.where(diag, g_col, 0.0), axis=0, keepdims=True)
                decay.append(jnp.where(incl, jnp.exp(jnp.where(incl, g_col - g_row, 0.0)), 0.0))
                vb4 = v4 * beta4
                kbe4 = kb4 * eg4
                qe4 = q4 * eg4
                kd4 = k4 * egl4
                for b, hh in enumerate(hs):
                    hsl = slice(b * dk, (b + 1) * dk)
                    q.append(qe4[:, hsl])
                    kd_t.append(kd4[:, hsl].T.astype(BF16))
                    egl_last.append(_lane_col(eg_last, hh, lane[0:1]))
                for p in range(pk // 2):
                    a = slice(2 * p * dk, (2 * p + 1) * dk)
                    bsl = slice((2 * p + 1) * dk, (2 * p + 2) * dk)
                    rhs_pair.append(jnp.concatenate(
                        [jnp.concatenate([vb4[:, a], kbe4[:, a]], axis=1),
                         jnp.concatenate([vb4[:, bsl], kbe4[:, bsl]], axis=1)], axis=0).astype(BF16))
        n_grp = len(kq)
        t_inv = _unit_lower_inverse([jnp.where(strict, kq[i][:c] * decay[i], 0.0) for i in range(n_grp)],
                                    eye, blk, pk)
        qk_p = [kq[i][c:] * decay[i] for i in range(n_grp)]

        def head_lhs(packed, i_grp, b):
            tile = packed[i_grp][:, (b // 2) * LANES:(b // 2 + 1) * LANES]
            return jnp.where(side == b % 2, tile, 0.0).astype(BF16)

        uw, qk = [], []
        for i_grp in range(n_grp):
            for b in range(pk):
                uw.append(_dot(head_lhs(t_inv, i_grp, b), rhs_pair[i_grp * (pk // 2) + b // 2]))
                qk.append(head_lhs(qk_p, i_grp, b))
        n = len(uw)
        wq = [jnp.concatenate([uw[i][:, dk:], q[i]], axis=0).astype(BF16) for i in range(n)]

        s = [s_ref[hh] for hh in heads]
        for j in range(nc):
            idx = [j * N_HEADS_GDN + hh for hh in heads]
            ws_qs = [_dot(wq[i], s[hh].astype(BF16)) for hh, i in zip(heads, idx)]
            v_new = [(uw[i][:, :dk] - ws_qs[hh][:c]).astype(BF16) for hh, i in zip(heads, idx)]
            for hh, i in zip(heads, idx):
                pair = jnp.concatenate([v_new[hh - hh % 2], v_new[hh - hh % 2 + 1]], axis=0)
                o = ws_qs[hh][c:] + _dot(qk[i], pair)
                o_ref[0, pl.ds(r0[j], c), sl[hh]] = o.astype(o_ref.dtype)
            s = [s[hh] * egl_last[i] + _dot(kd_t[i], v_new[hh]) for hh, i in zip(heads, idx)]
        for hh in heads:
            s_ref[hh] = s[hh]
        return carry

    lax.fori_loop(0, lb // (c * nc), body, 0)


def _gdn_chunk(q, k, v, gb):
    b, l, d = q.shape
    lb = min(GDN_BLOCK, l)
    row = lambda i, j: (i, j, 0)
    return pl.pallas_call(
        _gdn_chunk_kernel,
        grid=(b, l // lb),
        in_specs=[pl.BlockSpec((1, lb, d), row)] * 3 + [pl.BlockSpec((1, lb, LANES), row)],
        out_specs=pl.BlockSpec((1, lb, d), row),
        out_shape=jax.ShapeDtypeStruct((b, l, d), F32),
        scratch_shapes=[pltpu.VMEM((N_HEADS_GDN, HEAD_DIM_GDN, HEAD_DIM_GDN), F32)],
        compiler_params=_params(("arbitrary", "arbitrary")),
        name="gdn_chunk",
    )(q, k, v, gb)


def _fox_inproj_kernel(x_ref, sc_ref, sh_ref, w_ref, wf_ref, bf_ref,
                       q_ref, k_ref, v_ref, g_ref, cum_t_ref, carry_ref):
    tm = x_ref.shape[1]
    d = x_ref.shape[2]
    h = _rms_mod(x_ref[0], sc_ref[0], sh_ref[0])
    hb = h.astype(BF16)
    for s, o_ref in enumerate((q_ref, k_ref, v_ref)):
        o_ref[0] = _dot(hb, w_ref[:, s * d:(s + 1) * d]).astype(o_ref.dtype)
    g_ref[0] = _sigmoid(_dot(hb, w_ref[:, 3 * d:4 * d])).astype(g_ref.dtype)

    @pl.when(pl.program_id(1) == 0)
    def _():
        carry_ref[...] = jnp.zeros(carry_ref.shape, F32)

    f_logit = _dot_split_weights(h, wf_ref[...], LANES) + bf_ref[...]
    log_f = -_softplus(-f_logit)
    row = lax.broadcasted_iota(jnp.int32, (tm, tm), 0)
    col = lax.broadcasted_iota(jnp.int32, (tm, tm), 1)
    cum = _cumsum_rows((row >= col).astype(BF16), log_f) + carry_ref[...]
    carry_ref[...] = cum[tm - 1:tm, :]
    cum_t_ref[0] = cum.T


def _fox_inproj(x, sc, sh, w_main, w_f, b_f):
    b, l, d = x.shape
    tm = min(ROW_TILE, l)
    act = jax.ShapeDtypeStruct((b, l, d), BF16)
    row = lambda i, j: (i, j, 0)
    vec = lambda i, j: (i, 0, 0)
    const = lambda i, j: (0, 0)
    return pl.pallas_call(
        _fox_inproj_kernel,
        grid=(b, l // tm),
        in_specs=[
            pl.BlockSpec((1, tm, d), row),
            pl.BlockSpec((1, 1, d), vec),
            pl.BlockSpec((1, 1, d), vec),
            pl.BlockSpec(w_main.shape, const),
            pl.BlockSpec(w_f.shape, const),
            pl.BlockSpec(b_f.shape, const),
        ],
        out_specs=[pl.BlockSpec((1, tm, d), row)] * 4 + [pl.BlockSpec((1, LANES, tm), lambda i, j: (i, 0, j))],
        out_shape=[act, act, act, act, jax.ShapeDtypeStruct((b, LANES, l), F32)],
        scratch_shapes=[pltpu.VMEM((1, LANES), F32)],
        compiler_params=_params(("arbitrary", "arbitrary")),
        name="fox_inproj",
    )(x, sc, sh, w_main, w_f, b_f)


def _fox_attn_kernel(q_ref, k_ref, v_ref, cum_ref, qg_ref, kg_ref, o_ref, kn_ref, s_ref, mx_ref, ls_ref, acc_ref):
    tq = s_ref.shape[2]
    hd = HEAD_DIM_FOX
    lane = lax.broadcasted_iota(jnp.int32, (1, LANES), 1)
    first = lane < hd
    n_col = tq // LANES

    def head_norm(t, gain):
        sq = t * t
        s0 = jnp.sum(jnp.where(first, sq, 0.0), axis=1, keepdims=True)
        s1 = jnp.sum(jnp.where(first, 0.0, sq), axis=1, keepdims=True)
        ms = jnp.where(first, s0, s1) * (1.0 / hd)
        return t * lax.rsqrt(ms + EPS) * gain

    kn_ref[...] = head_norm(k_ref[0].astype(F32), kg_ref[...]).astype(kn_ref.dtype)

    def q_tile(qi, carry):
        q0 = qi * tq
        qn = head_norm(q_ref[0, pl.ds(q0, tq), :].astype(F32), qg_ref[...]) * (hd ** -0.5 * LOG2E)
        q2 = jnp.concatenate([jnp.where(first, qn, 0.0), jnp.where(first, 0.0, qn)], axis=0).astype(BF16)

        def scores(j, masked):
            k0 = j * tq if isinstance(j, int) else pl.multiple_of(j * tq, tq)
            ck = cum_ref[0, 0, :, pl.ds(k0, tq)] * LOG2E
            s = _dot_nt(q2, kn_ref[pl.ds(k0, tq), :])
            s = jnp.concatenate([s[:tq] - ck[0:1], s[tq:] - ck[1:2]], axis=0)
            if masked:
                r = lax.broadcasted_iota(jnp.int32, (tq, tq), 0)
                c = lax.broadcasted_iota(jnp.int32, (tq, tq), 1)
                keep = jnp.concatenate([r >= c, r >= c], axis=0)
                s = jnp.where(keep, s, NEG_BIG)
            s_ref[j] = s
            mx = mx_ref[...]
            for t in range(n_col):
                mx = jnp.maximum(mx, s[:, t * LANES:(t + 1) * LANES])
            mx_ref[...] = mx

        def scores_step(j, c):
            scores(j, False)
            return c

        mx_ref[...] = jnp.full(mx_ref.shape, NEG_BIG, F32)
        if qi > 0:
            lax.fori_loop(0, qi, scores_step, 0, unroll=min(qi, ATTN_UNROLL))
        scores(qi, True)
        mx_ref[...] = jnp.broadcast_to(jnp.max(mx_ref[...], axis=1, keepdims=True), mx_ref.shape)

        acc_ref[...] = jnp.zeros(acc_ref.shape, F32)
        ls_ref[...] = jnp.zeros(ls_ref.shape, F32)

        def weighted_sum(j, c):
            k0 = j * tq if isinstance(j, int) else pl.multiple_of(j * tq, tq)
            m = mx_ref[...]
            s = s_ref[j]
            p = jnp.concatenate([jnp.exp2(s[:, t * LANES:(t + 1) * LANES] - m) for t in range(n_col)], axis=1)
            acc_ref[...] += _dot(p.astype(BF16), v_ref[0, pl.ds(k0, tq), :])
            ls = ls_ref[...]
            for t in range(n_col):
                ls = ls + p[:, t * LANES:(t + 1) * LANES]
            ls_ref[...] = ls
            return c

        lax.fori_loop(0, qi + 1, weighted_sum, 0, unroll=min(qi + 1, ATTN_UNROLL))

        out = acc_ref[...] / jnp.sum(ls_ref[...], axis=1, keepdims=True)
        o_ref[0, pl.ds(q0, tq), :] = jnp.where(first, out[:tq], out[tq:]).astype(o_ref.dtype)
        return carry

    for qi in range(q_ref.shape[1] // tq):
        q_tile(qi, 0)


def _fox_attn(q, k, v, cum, q_gain, k_gain):
    b, l, d = q.shape
    tq = min(ATTN_TILE, l)
    pairs = d // LANES
    return pl.pallas_call(
        _fox_attn_kernel,
        grid=(b, pairs),
        in_specs=[
            pl.BlockSpec((1, l, LANES), lambda i, p: (i, 0, p)),
            pl.BlockSpec((1, l, LANES), lambda i, p: (i, 0, p)),
            pl.BlockSpec((1, l, LANES), lambda i, p: (i, 0, p)),
            pl.BlockSpec((1, 1, 2, l), lambda i, p: (i, p, 0, 0)),
            pl.BlockSpec((1, LANES), lambda i, p: (0, 0)),
            pl.BlockSpec((1, LANES), lambda i, p: (0, 0)),
        ],
        out_specs=pl.BlockSpec((1, l, LANES), lambda i, p: (i, 0, p)),
        out_shape=jax.ShapeDtypeStruct((b, l, d), BF16),
        scratch_shapes=[
            pltpu.VMEM((l, LANES), BF16),
            pltpu.VMEM((l // tq, 2 * tq, tq), F32),
            pltpu.VMEM((2 * tq, LANES), F32),
            pltpu.VMEM((2 * tq, LANES), F32),
            pltpu.VMEM((2 * tq, LANES), F32),
        ],
        compiler_params=_params(("arbitrary", "arbitrary")),
        name="fox_attn",
    )(q, k, v, cum, q_gain, k_gain)


def _outproj_kernel(head_norm, o_ref, gate_ref, gain_ref, w_ref, x_ref, g1_ref, sc_ref, sh_ref,
                    wr_ref, br_ref, x1_ref, h2_ref, meta_ref, wt_ref, counts_ref):
    tm = x_ref.shape[1]

    @pl.when((pl.program_id(0) == 0) & (pl.program_id(1) == 0))
    def _():
        counts_ref[...] = jnp.zeros(counts_ref.shape, F32)

    o = o_ref[0].astype(F32)
    gate = gate_ref[0].astype(F32)
    if head_norm:
        parts = []
        for hh in range(N_HEADS_GDN):
            seg = o[:, hh * HEAD_DIM_GDN:(hh + 1) * HEAD_DIM_GDN]
            parts.append(seg * lax.rsqrt(jnp.mean(seg * seg, axis=-1, keepdims=True) + EPS))
        o = jnp.concatenate(parts, axis=1) * gain_ref[...]
    y = _dot((o * gate).astype(BF16), w_ref[...])
    x1 = x_ref[0] + g1_ref[0] * y
    x1_ref[0] = x1
    h2 = _rms_mod(x1, sc_ref[0], sh_ref[0])
    h2_ref[0] = _pack_bf16_pairs(h2)
    lg = _dot_split_weights(h2, wr_ref[...], LANES) + br_ref[...]
    row = lax.broadcasted_iota(jnp.int32, (tm, tm), 0)
    col = lax.broadcasted_iota(jnp.int32, (tm, tm), 1)
    meta, wt, counts = _route_tile(lg, (row > col).astype(BF16), counts_ref[...])
    meta_ref[0] = meta.T[:SUBLANES, :]
    wt_ref[0] = wt
    counts_ref[...] = counts


def _outproj(head_norm, o, gate, gain, w_out, x, g1, sc2, sh2, w_route, b_route):
    b, l, d = x.shape
    tm = min(ROW_TILE, l)
    row = lambda i, j: (i, j, 0)
    vec = lambda i, j: (i, 0, 0)
    const = lambda i, j: (0, 0)
    return pl.pallas_call(
        functools.partial(_outproj_kernel, head_norm),
        grid=(b, l // tm),
        in_specs=[
            pl.BlockSpec((1, tm, d), row),
            pl.BlockSpec((1, tm, d), row),
            pl.BlockSpec(gain.shape, const),
            pl.BlockSpec(w_out.shape, const),
            pl.BlockSpec((1, tm, d), row),
            pl.BlockSpec((1, 1, d), vec),
            pl.BlockSpec((1, 1, d), vec),
            pl.BlockSpec((1, 1, d), vec),
            pl.BlockSpec(w_route.shape, const),
            pl.BlockSpec(b_route.shape, const),
        ],
        out_specs=[pl.BlockSpec((1, tm, d), row), pl.BlockSpec((1, tm, d // 2), row),
                   pl.BlockSpec((1, SUBLANES, tm), lambda i, j: (i, 0, j)), pl.BlockSpec((1, tm, LANES), row),
                   pl.BlockSpec((1, LANES), const)],
        out_shape=[jax.ShapeDtypeStruct((b, l, d), F32), jax.ShapeDtypeStruct((b, l, d // 2), jnp.uint32),
                   jax.ShapeDtypeStruct((b, SUBLANES, l), jnp.int32), jax.ShapeDtypeStruct((b, l, LANES), F32),
                   jax.ShapeDtypeStruct((1, LANES), F32)],
        compiler_params=_params(("arbitrary", "arbitrary")),
        name="outproj",
    )(o, gate, gain, w_out, x, g1, sc2, sh2, w_route, b_route)


def _route_tile(lg, tri_strict, counts):
    lane_i = lax.broadcasted_iota(jnp.int32, lg.shape, 1)
    lane = lane_i.astype(F32)
    big = 1e9

    def first_argmax(v, vmax):
        return jnp.min(jnp.where(v == vmax, lane, big), axis=1, keepdims=True)

    is_group = lane < N_GROUPS
    gl = jnp.where(is_group, lg, NEG_BIG)
    g_max = jnp.max(gl, axis=1, keepdims=True)
    g_sum = jnp.sum(jnp.where(is_group, jnp.exp(gl - g_max), 0.0), axis=1, keepdims=True)
    g_p = 1.0 / g_sum
    g_idx = first_argmax(gl, g_max)
    lo = N_GROUPS + EXPERTS_PER_GROUP * g_idx
    el = jnp.where((lane >= lo) & (lane < lo + EXPERTS_PER_GROUP), lg, NEG_BIG)
    m1 = jnp.max(el, axis=1, keepdims=True)
    i1 = first_argmax(el, m1)
    el2 = jnp.where(lane == i1, NEG_BIG, el)
    m2 = jnp.max(el2, axis=1, keepdims=True)
    i2 = first_argmax(el2, m2)
    e = jnp.exp(m2 - m1)
    w1 = g_p / (1.0 + e)
    w2 = w1 * e
    e1 = i1 - N_GROUPS
    e2 = i2 - N_GROUPS
    hot1 = (lane == e1).astype(F32)
    hot2 = (lane == e2).astype(F32)
    hot = hot1 + hot2
    before = _dot(tri_strict, hot.astype(BF16)) + counts
    rank1 = jnp.sum(before * hot1, axis=1, keepdims=True)
    rank2 = jnp.sum(before * hot2, axis=1, keepdims=True)
    meta = jnp.where(lane_i == 0, e1, jnp.where(lane_i == 1, e2, jnp.where(lane_i == 2, rank1, rank2)))
    return (meta.astype(jnp.int32), jnp.where(lane_i == 0, w1, w2),
            counts + jnp.sum(hot, axis=0, keepdims=True))


def _index_windows(idx):
    n = idx.shape[0]
    return jnp.zeros((n // SC_WINDOW, LANES), jnp.int32).at[:, :SC_WINDOW].set(
        idx.reshape(n // SC_WINDOW, SC_WINDOW))


def _sc_mesh():
    return plsc.VectorSubcoreMesh(core_axis_name="core", subcore_axis_name="subcore")


def _scatter_rows(x, idx, n_rows):
    n_in, d = x.shape
    n_idx = idx.shape[0]

    def program(x_hbm, i_hbm, o_hbm):
        def window(x_vmem, i_vmem):
            pltpu.sync_copy(x_vmem, o_hbm.at[i_vmem.at[0, pl.ds(0, SC_WINDOW)]])

        pltpu.emit_pipeline(
            window,
            grid=(n_idx // SC_WINDOW,),
            in_specs=[pl.BlockSpec((SC_WINDOW, d), lambda i: (i % (n_in // SC_WINDOW), 0)),
                      pl.BlockSpec((1, LANES), lambda i: (i, 0))],
            out_specs=[],
            core_axis_name=("core", "subcore"),
            dimension_semantics=(pltpu.PARALLEL,),
        )(x_hbm, i_hbm)

    return pl.kernel(program, out_type=jax.ShapeDtypeStruct((n_rows, d), x.dtype), mesh=_sc_mesh(),
                     name="moe_scatter_rows")(x, _index_windows(idx))


def _gather_rows(x, idx):
    d = x.shape[1]
    n_idx = idx.shape[0]

    def program(x_hbm, i_hbm, o_hbm):
        def window(i_vmem, o_vmem):
            pltpu.sync_copy(x_hbm.at[i_vmem.at[0, pl.ds(0, SC_WINDOW)]], o_vmem)

        pltpu.emit_pipeline(
            window,
            grid=(n_idx // SC_WINDOW,),
            in_specs=[pl.BlockSpec((1, LANES), lambda i: (i, 0))],
            out_specs=[pl.BlockSpec((SC_WINDOW, d), lambda i: (i, 0))],
            core_axis_name=("core", "subcore"),
            dimension_semantics=(pltpu.PARALLEL,),
        )(i_hbm, o_hbm)

    return pl.kernel(program, out_type=jax.ShapeDtypeStruct((n_idx, d), x.dtype), mesh=_sc_mesh(),
                     name="moe_gather_rows")(x, _index_windows(idx))


def _gmm_kernel(first_ref, count_ref, xs_ref, wgu_ref, wd_ref, ys_ref, wgu_bf, wd_bf, xbuf, obuf, sem_in, sem_out):
    e = pl.program_id(0)
    last = pl.num_programs(0) - 1
    slots = xbuf.shape[0]
    tm = xbuf.shape[1]
    half = xbuf.shape[2]
    f = wd_bf.shape[0]
    n_tiles = count_ref[e]
    first = first_ref[e]
    total = first_ref[last] + count_ref[last]

    def rows(g):
        return pl.ds(pl.multiple_of(g * tm, tm), tm)

    def load(g):
        return pltpu.make_async_copy(xs_ref.at[rows(g)], xbuf.at[g % slots], sem_in.at[g % slots])

    def store(g):
        return pltpu.make_async_copy(obuf.at[g % slots], ys_ref.at[rows(g)], sem_out.at[g % slots])

    @pl.when(e == 0)
    def _():
        for g in range(slots - 1):
            @pl.when(g < total)
            def _():
                load(g).start()

    @pl.when(n_tiles > 0)
    def _():
        wgu_bf[...] = wgu_ref[0].astype(BF16)
        wd_bf[...] = wd_ref[0].astype(BF16)

    def tile(t, carry):
        g = first + t
        load(g).wait()

        @pl.when(g + slots - 1 < total)
        def _():
            load(g + slots - 1).start()

        @pl.when(g >= slots)
        def _():
            store(g - slots).wait()

        x_hi, x_lo = _unpack_bf16_pairs(xbuf[g % slots])
        hu = _dot(x_hi.astype(BF16), wgu_bf[:half, :]) + _dot(x_lo.astype(BF16), wgu_bf[half:, :])
        act = _silu(hu[:, :f]) * hu[:, f:]
        obuf[g % slots] = _pack_bf16_pairs(_dot(act.astype(BF16), wd_bf[...]))
        store(g).start()
        return carry

    lax.fori_loop(0, n_tiles, tile, 0)

    @pl.when(e == last)
    def _():
        for j in range(slots):
            g = total - slots + j

            @pl.when(g >= 0)
            def _():
                store(g).wait()


def _gmm(layer, tile_first, tile_count, xs, w_gate_up, w_down):
    n_rows, half = xs.shape
    d = 2 * half
    tm = MOE_TILE
    f2 = w_gate_up.shape[2]
    grid_spec = pltpu.PrefetchScalarGridSpec(
        num_scalar_prefetch=2,
        grid=(N_EXPERTS,),
        in_specs=[
            pl.BlockSpec(memory_space=pl.ANY),
            pl.BlockSpec((1, d, f2), lambda e, tf, tc: (layer * N_EXPERTS + e, 0, 0)),
            pl.BlockSpec((1, f2 // 2, d), lambda e, tf, tc: (layer * N_EXPERTS + e, 0, 0)),
        ],
        out_specs=pl.BlockSpec(memory_space=pl.ANY),
        scratch_shapes=[
            pltpu.VMEM((d, f2), BF16), pltpu.VMEM((f2 // 2, d), BF16),
            pltpu.VMEM((GMM_SLOTS, tm, half), jnp.uint32), pltpu.VMEM((GMM_SLOTS, tm, half), jnp.uint32),
            pltpu.SemaphoreType.DMA((GMM_SLOTS,)), pltpu.SemaphoreType.DMA((GMM_SLOTS,)),
        ],
    )
    return pl.pallas_call(
        _gmm_kernel,
        grid_spec=grid_spec,
        out_shape=jax.ShapeDtypeStruct((n_rows, half), jnp.uint32),
        input_output_aliases={2: 0},
        compiler_params=_params(("arbitrary",)),
        name="moe_gmm",
    )(tile_first, tile_count, xs, w_gate_up, w_down)


def _combine_kernel(final, y0_ref, y1_ref, wt_ref, x_ref, g2_ref, fg_ref, o_ref):
    wt = wt_ref[...]
    lane = lax.broadcasted_iota(jnp.int32, wt.shape, 1)
    w0 = _lane_col(wt, 0, lane)
    w1 = _lane_col(wt, 1, lane)
    a_hi, a_lo = _unpack_bf16_pairs(y0_ref[...])
    b_hi, b_lo = _unpack_bf16_pairs(y1_ref[...])
    y = jnp.concatenate([w0 * a_hi + w1 * b_hi, w0 * a_lo + w1 * b_lo], axis=1)
    x2 = x_ref[...] + g2_ref[0] * y
    if final:
        x2 = x2 * lax.rsqrt(jnp.mean(x2 * x2, axis=-1, keepdims=True) + EPS) * fg_ref[...]
    o_ref[...] = x2


def _combine(final, yg, wt, x1, g2, final_gain, tokens_per_seq):
    t, d = x1.shape
    tt = min(TOKEN_TILE, tokens_per_seq)
    nb = t // tt
    per_seq = tokens_per_seq // tt
    return pl.pallas_call(
        functools.partial(_combine_kernel, final),
        grid=(nb,),
        in_specs=[
            pl.BlockSpec((tt, d // 2), lambda i: (i, 0)),
            pl.BlockSpec((tt, d // 2), lambda i: (i + nb, 0)),
            pl.BlockSpec((tt, LANES), lambda i: (i, 0)),
            pl.BlockSpec((tt, d), lambda i: (i, 0)),
            pl.BlockSpec((1, 1, d), lambda i: (i // per_seq, 0, 0)),
            pl.BlockSpec((1, d), lambda i: (0, 0)),
        ],
        out_specs=pl.BlockSpec((tt, d), lambda i: (i, 0)),
        out_shape=jax.ShapeDtypeStruct((t, d), F32),
        compiler_params=_params(("arbitrary",)),
        name="moe_combine",
    )(yg, yg, wt, x1, g2, final_gain)


def _sorted_positions(meta, counts, tile):
    counts = counts[0, :N_EXPERTS].astype(jnp.int32)
    tiles_per = (counts + tile - 1) // tile
    tile_first = jnp.cumsum(tiles_per) - tiles_per
    choice_major = lambda a: jnp.transpose(a, (1, 0, 2)).reshape(TOP_K, -1)
    eid = choice_major(meta[:, :TOP_K, :])
    rank = choice_major(meta[:, TOP_K:2 * TOP_K, :])
    onehot = (eid[..., None] == jnp.arange(N_EXPERTS, dtype=jnp.int32)).astype(jnp.int32)
    pos = jnp.sum(onehot * (tile_first * tile), axis=-1) + rank
    return pos.astype(jnp.int32), tile_first.astype(jnp.int32), tiles_per.astype(jnp.int32)


def _sorted_rows(n_tokens, tile):
    return (TOP_K * n_tokens // tile + N_EXPERTS) * tile


def _hier_moe(final, layer, h2, meta, wt, counts, x1, g2, w_gate_up, w_down, final_gain, tokens_per_seq):
    pos, tile_first, tile_count = _sorted_positions(meta, counts, MOE_TILE)
    pos = pos.reshape(-1)
    xs = _scatter_rows(h2, pos, _sorted_rows(h2.shape[0], MOE_TILE))
    ys = _gmm(layer, tile_first, tile_count, xs, w_gate_up, w_down)
    return _combine(final, _gather_rows(ys, pos), wt, x1, g2, final_gain, tokens_per_seq)


def _hi_lo_cols(w, n):
    k, m = w.shape
    wp = jnp.zeros((k, n), F32).at[:, :m].set(w)
    hi = wp.astype(BF16)
    lo = (wp - hi.astype(F32)).astype(BF16)
    return jnp.concatenate([hi, lo], axis=1)


def _pad_row(v, n):
    return jnp.zeros((1, n), F32).at[0, :v.shape[0]].set(v)


def kernel(x, c, ada_w, ada_b, gdn_w_in, gdn_conv, gdn_a_log, gdn_dt_bias, gdn_norm, gdn_w_out,
           fox_w_in, fox_b_f, fox_q_norm, fox_k_norm, fox_w_out,
           moe_w_group, moe_b_group, moe_w_router, moe_b_router, moe_w_gate_up, moe_w_down,
           final_norm):
    b, l, d = x.shape
    depth = ada_w.shape[0]
    t = b * l
    ada = _ada(c, ada_w, ada_b)
    w_gate_up = moe_w_gate_up.reshape(depth * N_EXPERTS, d, moe_w_gate_up.shape[-1])
    w_down = moe_w_down.reshape(depth * N_EXPERTS, moe_w_down.shape[-2], d)
    final_gain = final_norm.reshape(1, d)

    for i in range(depth):
        sh1, sc1, g1, sh2, sc2, g2 = [ada[i, :, s * d:(s + 1) * d].reshape(b, 1, d) for s in range(6)]
        j = i // 2
        w_route = _hi_lo_cols(jnp.concatenate([moe_w_group[i], moe_w_router[i]], axis=1), LANES)
        b_route = _pad_row(jnp.concatenate([moe_b_group[i], moe_b_router[i]]), LANES)
        if i % 2 == 0:
            w_in = gdn_w_in[j]
            q, k, v, gate, gb = _gdn_inproj(
                x, sc1, sh1, w_in.astype(BF16), _hi_lo_cols(w_in[:, 4 * d:], LANES),
                gdn_conv[j], jnp.concatenate([_pad_row(gdn_a_log[j], LANES), _pad_row(gdn_dt_bias[j], LANES)]))
            o = _gdn_chunk(q, k, v, gb)
            gain = jnp.tile(gdn_norm[j], N_HEADS_GDN).reshape(1, d)
            x1, h2, meta, wt, counts = _outproj(True, o, gate, gain, gdn_w_out[j].astype(BF16), x, g1, sc2, sh2,
                                      w_route, b_route)
        else:
            w_in = fox_w_in[j]
            q, k, v, gate, cum_t = _fox_inproj(
                x, sc1, sh1, w_in.astype(BF16), _hi_lo_cols(w_in[:, 4 * d:], LANES),
                _pad_row(fox_b_f[j], LANES))
            cum = cum_t[:, :N_HEADS_FOX, :].reshape(b, N_HEADS_FOX // 2, 2, l)
            q_gain = jnp.tile(fox_q_norm[j], 2).reshape(1, LANES)
            k_gain = jnp.tile(fox_k_norm[j], 2).reshape(1, LANES)
            o = _fox_attn(q, k, v, cum, q_gain, k_gain)
            x1, h2, meta, wt, counts = _outproj(False, o, gate, final_gain, fox_w_out[j].astype(BF16), x, g1, sc2, sh2,
                                      w_route, b_route)
        x = _hier_moe(i == depth - 1, i, h2.reshape(t, d // 2), meta, wt.reshape(t, LANES), counts,
                      x1.reshape(t, d), g2, w_gate_up, w_down, final_gain, l).reshape(b, l, d)
    return x
```

```python
import functools

import jax
import jax.numpy as jnp
from jax import lax
from jax.experimental import pallas as pl
from jax.experimental.pallas import tpu as pltpu
from jax.experimental.pallas import tpu_sc as plsc

F32 = jnp.float32
BF16 = jnp.bfloat16

EPS = 1e-6
CHUNK = 64
N_HEADS_GDN = 8
HEAD_DIM_GDN = 128
N_HEADS_FOX = 16
HEAD_DIM_FOX = 64
CONV_K = 4
N_GROUPS = 4
EXPERTS_PER_GROUP = 8
N_EXPERTS = N_GROUPS * EXPERTS_PER_GROUP
TOP_K = 2

LANES = 128
SUBLANES = 8
VMEM_LIMIT_BYTES = 56 * 1024 * 1024

ROW_TILE = 512
ATTN_TILE = 512
ATTN_UNROLL = 2
GDN_BLOCK = 512
GDN_CHUNKS_PER_ITER = 4
GDN_HEADS_PACKED = 4
MOE_TILE = 256
GMM_SLOTS = 4
TOKEN_TILE = 512
SC_WINDOW = 64
NEG_BIG = -1e30
LOG2E = 1.4426950408889634


def _params(semantics):
    return pltpu.CompilerParams(dimension_semantics=semantics, vmem_limit_bytes=VMEM_LIMIT_BYTES)


def _sigmoid(x):
    return 0.5 * jnp.tanh(0.5 * x) + 0.5


def _silu(x):
    return x * _sigmoid(x)


def _softplus(x):
    return jnp.maximum(x, 0.0) + jnp.log(1.0 + jnp.exp(-jnp.abs(x)))


def _split2(a):
    hi = a.astype(BF16)
    lo = (a - hi.astype(F32)).astype(BF16)
    return hi, lo


def _split3(a):
    hi = a.astype(BF16)
    r = a - hi.astype(F32)
    mid = r.astype(BF16)
    lo = (r - mid.astype(F32)).astype(BF16)
    return hi, mid, lo


def _dot(a, b):
    return jnp.dot(a, b, preferred_element_type=F32)


def _dot_nt(a, b):
    return lax.dot_general(a, b, (((1,), (1,)), ((), ())), preferred_element_type=F32)


def _dot_split_weights(a, w_hi_lo, n):
    a_hi, a_lo = _split2(a)
    r = _dot(a_hi, w_hi_lo)
    return r[:, :n] + r[:, n:] + _dot(a_lo, w_hi_lo[:, :n])


def _cumsum_rows(tri_bf16, v):
    hi, mid, lo = _split3(v)
    r = _dot(tri_bf16, jnp.concatenate([hi, mid], axis=1))
    return r[:, :LANES] + r[:, LANES:] + _dot(tri_bf16, lo)


def _rms_mod(x, sc, sh):
    ms = jnp.mean(x * x, axis=-1, keepdims=True)
    return x * lax.rsqrt(ms + EPS) * (1.0 + sc) + sh


def _pack_bf16_pairs(a):
    n = a.shape[1] // 2
    bits = lax.bitcast_convert_type(a.astype(BF16).astype(F32), jnp.uint32)
    return (bits[:, :n] & jnp.uint32(0xFFFF0000)) | (bits[:, n:] >> 16)


def _unpack_bf16_pairs(p):
    hi = lax.bitcast_convert_type(p & jnp.uint32(0xFFFF0000), F32)
    lo = lax.bitcast_convert_type(p << 16, F32)
    return hi, lo


def _lane_col(v, idx, lane):
    return jnp.sum(jnp.where(lane == idx, v, 0.0), axis=1, keepdims=True)


def _ada_kernel(c_ref, w_ref, b_ref, o_ref):
    c = c_ref[...]
    o_ref[0] = _dot(_silu(c).astype(BF16), w_ref[0].astype(BF16)) + b_ref[0]


def _ada(c, ada_w, ada_b):
    depth, d, n = ada_w.shape
    b = c.shape[0]
    tn = 1536
    return pl.pallas_call(
        _ada_kernel,
        grid=(depth, n // tn),
        in_specs=[
            pl.BlockSpec((b, d), lambda i, j: (0, 0)),
            pl.BlockSpec((1, d, tn), lambda i, j: (i, 0, j)),
            pl.BlockSpec((1, 1, tn), lambda i, j: (i, 0, j)),
        ],
        out_specs=pl.BlockSpec((1, b, tn), lambda i, j: (i, 0, j)),
        out_shape=jax.ShapeDtypeStruct((depth, b, n), F32),
        compiler_params=_params(("arbitrary", "arbitrary")),
        name="ada",
    )(c, ada_w, ada_b.reshape(depth, 1, n))


def _combine_rows(y0, y1, wt):
    lane = lax.broadcasted_iota(jnp.int32, wt.shape, 1)
    w0 = _lane_col(wt, 0, lane)
    w1 = _lane_col(wt, 1, lane)
    a_hi, a_lo = _unpack_bf16_pairs(y0)
    b_hi, b_lo = _unpack_bf16_pairs(y1)
    return jnp.concatenate([w0 * a_hi + w1 * b_hi, w0 * a_lo + w1 * b_lo], axis=1)


N_PENDING_REFS = 5


def _residual_tile(pending, refs):
    if not pending:
        return refs[0][0]
    x1_ref, y0_ref, y1_ref, wt_ref, g2_ref = refs
    return x1_ref[0] + g2_ref[0] * _combine_rows(y0_ref[...], y1_ref[...], wt_ref[...])


def _gdn_inproj_kernel(pending, *refs):
    nx = N_PENDING_REFS if pending else 1
    sc_ref, sh_ref, w_ref, wab_ref, conv_ref, prm_ref, q_ref, k_ref, v_ref, gate_ref, gb_ref = refs[nx:nx + 11]
    halo_ref = refs[-1]
    x = _residual_tile(pending, refs[:nx])
    if pending:
        refs[nx + 11][0] = x
    tm, d = x.shape
    h = _rms_mod(x, sc_ref[0], sh_ref[0])
    hb = h.astype(BF16)

    @pl.when(pl.program_id(1) == 0)
    def _():
        halo_ref[:, 0:SUBLANES, :] = jnp.zeros((3, SUBLANES, d), F32)

    for s, o_ref in enumerate((q_ref, k_ref, v_ref)):
        raw = _dot(hb, w_ref[:, s * d:(s + 1) * d])
        halo_ref[s, SUBLANES:SUBLANES + tm, :] = raw
        cw = conv_ref[:, s * d:(s + 1) * d]
        y = raw * cw[3:4]
        for j in range(CONV_K - 1):
            off = SUBLANES - (CONV_K - 1) + j
            y = y + halo_ref[s, off:off + tm, :] * cw[j:j + 1]
        halo_ref[s, 0:SUBLANES, :] = halo_ref[s, tm:tm + SUBLANES, :]
        y = _silu(y)
        if s < 2:
            scale = HEAD_DIM_GDN ** -0.5 if s == 0 else 1.0
            for hh in range(N_HEADS_GDN):
                seg = y[:, hh * HEAD_DIM_GDN:(hh + 1) * HEAD_DIM_GDN]
                inv = lax.rsqrt(jnp.sum(seg * seg, axis=-1, keepdims=True) + EPS) * scale
                o_ref[0, :, hh * HEAD_DIM_GDN:(hh + 1) * HEAD_DIM_GDN] = (seg * inv).astype(o_ref.dtype)
        else:
            o_ref[0] = y.astype(o_ref.dtype)

    gate_ref[0] = _silu(_dot(hb, w_ref[:, 3 * d:4 * d])).astype(gate_ref.dtype)

    ab = _dot_split_weights(h, wab_ref[...], LANES)
    log_alpha = -jnp.exp(prm_ref[0:1, :]) * _softplus(ab + prm_ref[1:2, :])
    lane = lax.broadcasted_iota(jnp.int32, ab.shape, 1)
    gb_ref[0] = jnp.where(lane < N_HEADS_GDN, log_alpha, _sigmoid(ab))


def _residual_operands(x_src, tm):
    row = lambda i, j: (i, j, 0)
    if not isinstance(x_src, tuple):
        b, l, d = x_src.shape
        return False, [x_src], [pl.BlockSpec((1, tm, d), row)], [], []
    x1, yg, wt, g2 = x_src
    b, l, d = x1.shape
    per_seq = l // tm
    n_tiles = b * per_seq
    specs = [
        pl.BlockSpec((1, tm, d), row),
        pl.BlockSpec((tm, d // 2), lambda i, j: (i * per_seq + j, 0)),
        pl.BlockSpec((tm, d // 2), lambda i, j: (i * per_seq + j + n_tiles, 0)),
        pl.BlockSpec((tm, LANES), lambda i, j: (i * per_seq + j, 0)),
        pl.BlockSpec((1, 1, d), lambda i, j: (i, 0, 0)),
    ]
    return (True, [x1, yg, yg, wt, g2], specs, [jax.ShapeDtypeStruct((b, l, d), F32)],
            [pl.BlockSpec((1, tm, d), row)])


def _gdn_inproj(x_src, sc, sh, w_main, w_ab, conv_w, prm):
    b, l, d = (x_src[0] if isinstance(x_src, tuple) else x_src).shape
    tm = min(ROW_TILE, l)
    pending, x_ops, x_specs, x_out_shape, x_out_spec = _residual_operands(x_src, tm)
    act = jax.ShapeDtypeStruct((b, l, d), BF16)
    row = lambda i, j: (i, j, 0)
    vec = lambda i, j: (i, 0, 0)
    const = lambda i, j: (0, 0)
    return pl.pallas_call(
        functools.partial(_gdn_inproj_kernel, pending),
        grid=(b, l // tm),
        in_specs=x_specs + [
            pl.BlockSpec((1, 1, d), vec),
            pl.BlockSpec((1, 1, d), vec),
            pl.BlockSpec(w_main.shape, const),
            pl.BlockSpec(w_ab.shape, const),
            pl.BlockSpec(conv_w.shape, const),
            pl.BlockSpec(prm.shape, const),
        ],
        out_specs=[pl.BlockSpec((1, tm, d), row)] * 4 + [pl.BlockSpec((1, tm, LANES), row)] + x_out_spec,
        out_shape=[act, act, act, act, jax.ShapeDtypeStruct((b, l, LANES), F32)] + x_out_shape,
        scratch_shapes=[pltpu.VMEM((3, tm + SUBLANES, d), F32)],
        compiler_params=_params(("arbitrary", "arbitrary")),
        name="gdn_inproj",
    )(*x_ops, sc, sh, w_main, w_ab, conv_w, prm)


def _block_diag(r, blk, n_blk):
    return jnp.concatenate([jnp.where(blk == b, r, jnp.zeros_like(r)) for b in range(n_blk)], axis=0)


def _mm_packed(lhs, rhs, blk, n_blk):
    m = lhs.shape[0]
    l_hi, l_lo = _split2(lhs)
    r_hi, r_lo = _split2(rhs)
    t = _dot(jnp.concatenate([l_hi, l_lo], axis=0), _block_diag(r_hi, blk, n_blk))
    return t[:m] + t[m:] + _dot(l_hi, _block_diag(r_lo, blk, n_blk))


def _unit_lower_inverse(a_strict, eye, blk, n_blk):
    n = a_strict[0].shape[0]
    levels = n.bit_length() - 1
    m = [-a for a in a_strict]
    x = [eye + mi for mi in m]
    p = [_mm_packed(mi, mi, blk, n_blk) for mi in m]
    for _ in range(levels - 2):
        r = [_mm_packed(jnp.concatenate([pi, xi], axis=0), pi, blk, n_blk) for pi, xi in zip(p, x)]
        x = [xi + ri[n:] for xi, ri in zip(x, r)]
        p = [ri[:n] for ri in r]
    return [xi + _mm_packed(xi, pi, blk, n_blk) for xi, pi in zip(x, p)]


def _gdn_chunk_kernel(q_ref, k_ref, v_ref, gb_ref, o_ref, s_ref):
    lb = q_ref.shape[1]
    c = CHUNK
    dk = HEAD_DIM_GDN
    nc = GDN_CHUNKS_PER_ITER
    heads = range(N_HEADS_GDN)
    sl = [slice(hh * dk, (hh + 1) * dk) for hh in heads]

    @pl.when(pl.program_id(1) == 0)
    def _():
        s_ref[...] = jnp.zeros(s_ref.shape, F32)

    pk = GDN_HEADS_PACKED
    groups = N_HEADS_GDN // pk
    tri = (lax.broadcasted_iota(jnp.int32, (c, c), 0) >= lax.broadcasted_iota(jnp.int32, (c, c), 1)).astype(BF16)
    lane = lax.broadcasted_iota(jnp.int32, (c, LANES), 1)
    side = lane // c
    row_p = lax.broadcasted_iota(jnp.int32, (c, pk * c), 0)
    lane_p = lax.broadcasted_iota(jnp.int32, (c, pk * c), 1)
    blk = lane_p // c
    col_p = lane_p - blk * c
    incl = row_p >= col_p
    strict = row_p > col_p
    diag = row_p == col_p
    eye = diag.astype(F32)
    blk_k = lax.broadcasted_iota(jnp.int32, (c, pk * dk), 1) // dk

    def per_head(cols, width):
        return jnp.concatenate([jnp.broadcast_to(col, (c, width)) for col in cols], axis=1)

    def body(it, carry):
        r0, q, kd_t, egl_last, rhs_pair, kq, decay, eg_cols = [], [], [], [], [], [], [], []
        for j in range(nc):
            r = pl.multiple_of((it * nc + j) * c, c)
            r0.append(r)
            gb = gb_ref[0, pl.ds(r, c), :]
            g = _cumsum_rows(tri, gb)
            g_last = g[c - 1:c, :]
            eg = jnp.exp(g)
            egl = jnp.exp(g_last - g)
            eg_last = jnp.exp(g_last)
            for gi in range(groups):
                hs = [gi * pk + b for b in range(pk)]
                cols = slice(gi * pk * dk, (gi + 1) * pk * dk)
                q4 = q_ref[0, pl.ds(r, c), cols].astype(F32)
                k4 = k_ref[0, pl.ds(r, c), cols].astype(F32)
                v4 = v_ref[0, pl.ds(r, c), cols].astype(F32)
                beta4 = per_head([_lane_col(gb, N_HEADS_GDN + hh, lane) for hh in hs], dk)
                eg4 = per_head([_lane_col(eg, hh, lane) for hh in hs], dk)
                egl4 = per_head([_lane_col(egl, hh, lane) for hh in hs], dk)
                kb4 = k4 * beta4
                k4b = k4.astype(BF16)
                k_diag = jnp.concatenate([jnp.where(blk_k == b, k4b, jnp.zeros_like(k4b)) for b in range(pk)],
                                         axis=0)
                kq.append(_dot_nt(jnp.concatenate([kb4, q4], axis=0).astype(BF16), k_diag))
                g_col = per_head([_lane_col(g, hh, lane) for hh in hs], c)
                g_row = jnp.sum(jnp.where(diag, g_col, 0.0), axis=0, keepdims=True)
                decay.append(jnp.where(incl, jnp.exp(jnp.where(incl, g_col - g_row, 0.0)), 0.0))
                vb4 = v4 * beta4
                kbe4 = kb4 * eg4
                qe4 = q4 * eg4
                kd4 = k4 * egl4
                for b, hh in enumerate(hs):
                    hsl = slice(b * dk, (b + 1) * dk)
                    q.append(qe4[:, hsl])
                    kd_t.append(kd4[:, hsl].T.astype(BF16))
                    egl_last.append(_lane_col(eg_last, hh, lane[0:1]))
                for p in range(pk // 2):
                    a = slice(2 * p * dk, (2 * p + 1) * dk)
                    bsl = slice((2 * p + 1) * dk, (2 * p + 2) * dk)
                    rhs_pair.append(jnp.concatenate(
                        [jnp.concatenate([vb4[:, a], kbe4[:, a]], axis=1),
                         jnp.concatenate([vb4[:, bsl], kbe4[:, bsl]], axis=1)], axis=0).astype(BF16))
        n_grp = len(kq)
        t_inv = _unit_lower_inverse([jnp.where(strict, kq[i][:c] * decay[i], 0.0) for i in range(n_grp)],
                                    eye, blk, pk)
        qk_p = [kq[i][c:] * decay[i] for i in range(n_grp)]

        def head_lhs(packed, i_grp, b):
            tile = packed[i_grp][:, (b // 2) * LANES:(b // 2 + 1) * LANES]
            return jnp.where(side == b % 2, tile, 0.0).astype(BF16)

        uw, qk = [], []
        for i_grp in range(n_grp):
            for b in range(pk):
                uw.append(_dot(head_lhs(t_inv, i_grp, b), rhs_pair[i_grp * (pk // 2) + b // 2]))
                qk.append(head_lhs(qk_p, i_grp, b))
        n = len(uw)
        wq = [jnp.concatenate([uw[i][:, dk:], q[i]], axis=0).astype(BF16) for i in range(n)]

        s = [s_ref[hh] for hh in heads]
        for j in range(nc):
            idx = [j * N_HEADS_GDN + hh for hh in heads]
            ws_qs = [_dot(wq[i], s[hh].astype(BF16)) for hh, i in zip(heads, idx)]
            v_new = [(uw[i][:, :dk] - ws_qs[hh][:c]).astype(BF16) for hh, i in zip(heads, idx)]
            for hh, i in zip(heads, idx):
                pair = jnp.concatenate([v_new[hh - hh % 2], v_new[hh - hh % 2 + 1]], axis=0)
                o = ws_qs[hh][c:] + _dot(qk[i], pair)
                o_ref[0, pl.ds(r0[j], c), sl[hh]] = o.astype(o_ref.dtype)
            s = [s[hh] * egl_last[i] + _dot(kd_t[i], v_new[hh]) for hh, i in zip(heads, idx)]
        for hh in heads:
            s_ref[hh] = s[hh]
        return carry

    lax.fori_loop(0, lb // (c * nc), body, 0)


def _gdn_chunk(q, k, v, gb):
    b, l, d = q.shape
    lb = min(GDN_BLOCK, l)
    row = lambda i, j: (i, j, 0)
    return pl.pallas_call(
        _gdn_chunk_kernel,
        grid=(b, l // lb),
        in_specs=[pl.BlockSpec((1, lb, d), row)] * 3 + [pl.BlockSpec((1, lb, LANES), row)],
        out_specs=pl.BlockSpec((1, lb, d), row),
        out_shape=jax.ShapeDtypeStruct((b, l, d), F32),
        scratch_shapes=[pltpu.VMEM((N_HEADS_GDN, HEAD_DIM_GDN, HEAD_DIM_GDN), F32)],
        compiler_params=_params(("arbitrary", "arbitrary")),
        name="gdn_chunk",
    )(q, k, v, gb)


def _fox_inproj_kernel(pending, *refs):
    nx = N_PENDING_REFS if pending else 1
    sc_ref, sh_ref, w_ref, wf_ref, bf_ref, q_ref, k_ref, v_ref, g_ref, cum_t_ref = refs[nx:nx + 10]
    carry_ref = refs[-1]
    x = _residual_tile(pending, refs[:nx])
    if pending:
        refs[nx + 10][0] = x
    tm, d = x.shape
    h = _rms_mod(x, sc_ref[0], sh_ref[0])
    hb = h.astype(BF16)
    for s, o_ref in enumerate((q_ref, k_ref, v_ref)):
        o_ref[0] = _dot(hb, w_ref[:, s * d:(s + 1) * d]).astype(o_ref.dtype)
    g_ref[0] = _sigmoid(_dot(hb, w_ref[:, 3 * d:4 * d])).astype(g_ref.dtype)

    @pl.when(pl.program_id(1) == 0)
    def _():
        carry_ref[...] = jnp.zeros(carry_ref.shape, F32)

    f_logit = _dot_split_weights(h, wf_ref[...], LANES) + bf_ref[...]
    log_f = -_softplus(-f_logit)
    row = lax.broadcasted_iota(jnp.int32, (tm, tm), 0)
    col = lax.broadcasted_iota(jnp.int32, (tm, tm), 1)
    cum = _cumsum_rows((row >= col).astype(BF16), log_f) + carry_ref[...]
    carry_ref[...] = cum[tm - 1:tm, :]
    cum_t_ref[0] = cum.T


def _fox_inproj(x_src, sc, sh, w_main, w_f, b_f):
    b, l, d = (x_src[0] if isinstance(x_src, tuple) else x_src).shape
    tm = min(ROW_TILE, l)
    pending, x_ops, x_specs, x_out_shape, x_out_spec = _residual_operands(x_src, tm)
    act = jax.ShapeDtypeStruct((b, l, d), BF16)
    row = lambda i, j: (i, j, 0)
    vec = lambda i, j: (i, 0, 0)
    const = lambda i, j: (0, 0)
    return pl.pallas_call(
        functools.partial(_fox_inproj_kernel, pending),
        grid=(b, l // tm),
        in_specs=x_specs + [
            pl.BlockSpec((1, 1, d), vec),
            pl.BlockSpec((1, 1, d), vec),
            pl.BlockSpec(w_main.shape, const),
            pl.BlockSpec(w_f.shape, const),
            pl.BlockSpec(b_f.shape, const),
        ],
        out_specs=([pl.BlockSpec((1, tm, d), row)] * 4 + [pl.BlockSpec((1, LANES, tm), lambda i, j: (i, 0, j))]
                   + x_out_spec),
        out_shape=[act, act, act, act, jax.ShapeDtypeStruct((b, LANES, l), F32)] + x_out_shape,
        scratch_shapes=[pltpu.VMEM((1, LANES), F32)],
        compiler_params=_params(("arbitrary", "arbitrary")),
        name="fox_inproj",
    )(*x_ops, sc, sh, w_main, w_f, b_f)


def _fox_attn_kernel(q_ref, k_ref, v_ref, cum_ref, qg_ref, kg_ref, o_ref, kn_ref, s_ref, mx_ref, ls_ref, acc_ref):
    tq = s_ref.shape[2]
    hd = HEAD_DIM_FOX
    lane = lax.broadcasted_iota(jnp.int32, (1, LANES), 1)
    first = lane < hd
    n_col = tq // LANES

    def head_norm(t, gain):
        sq = t * t
        s0 = jnp.sum(jnp.where(first, sq, 0.0), axis=1, keepdims=True)
        s1 = jnp.sum(jnp.where(first, 0.0, sq), axis=1, keepdims=True)
        ms = jnp.where(first, s0, s1) * (1.0 / hd)
        return t * lax.rsqrt(ms + EPS) * gain

    kn_ref[...] = head_norm(k_ref[0].astype(F32), kg_ref[...]).astype(kn_ref.dtype)

    def q_tile(qi, carry):
        q0 = qi * tq
        qn = head_norm(q_ref[0, pl.ds(q0, tq), :].astype(F32), qg_ref[...]) * (hd ** -0.5 * LOG2E)
        q2 = jnp.concatenate([jnp.where(first, qn, 0.0), jnp.where(first, 0.0, qn)], axis=0).astype(BF16)

        def scores(j, masked):
            k0 = j * tq if isinstance(j, int) else pl.multiple_of(j * tq, tq)
            ck = cum_ref[0, 0, :, pl.ds(k0, tq)] * LOG2E
            s = _dot_nt(q2, kn_ref[pl.ds(k0, tq), :])
            s = jnp.concatenate([s[:tq] - ck[0:1], s[tq:] - ck[1:2]], axis=0)
            if masked:
                r = lax.broadcasted_iota(jnp.int32, (tq, tq), 0)
                c = lax.broadcasted_iota(jnp.int32, (tq, tq), 1)
                keep = jnp.concatenate([r >= c, r >= c], axis=0)
                s = jnp.where(keep, s, NEG_BIG)
            s_ref[j] = s
            mx = mx_ref[...]
            for t in range(n_col):
                mx = jnp.maximum(mx, s[:, t * LANES:(t + 1) * LANES])
            mx_ref[...] = mx

        def scores_step(j, c):
            scores(j, False)
            return c

        mx_ref[...] = jnp.full(mx_ref.shape, NEG_BIG, F32)
        if qi > 0:
            lax.fori_loop(0, qi, scores_step, 0, unroll=min(qi, ATTN_UNROLL))
        scores(qi, True)
        mx_ref[...] = jnp.broadcast_to(jnp.max(mx_ref[...], axis=1, keepdims=True), mx_ref.shape)

        acc_ref[...] = jnp.zeros(acc_ref.shape, F32)
        ls_ref[...] = jnp.zeros(ls_ref.shape, F32)

        def weighted_sum(j, c):
            k0 = j * tq if isinstance(j, int) else pl.multiple_of(j * tq, tq)
            m = mx_ref[...]
            s = s_ref[j]
            p = jnp.concatenate([jnp.exp2(s[:, t * LANES:(t + 1) * LANES] - m) for t in range(n_col)], axis=1)
            acc_ref[...] += _dot(p.astype(BF16), v_ref[0, pl.ds(k0, tq), :])
            ls = ls_ref[...]
            for t in range(n_col):
                ls = ls + p[:, t * LANES:(t + 1) * LANES]
            ls_ref[...] = ls
            return c

        lax.fori_loop(0, qi + 1, weighted_sum, 0, unroll=min(qi + 1, ATTN_UNROLL))

        out = acc_ref[...] / jnp.sum(ls_ref[...], axis=1, keepdims=True)
        o_ref[0, pl.ds(q0, tq), :] = jnp.where(first, out[:tq], out[tq:]).astype(o_ref.dtype)
        return carry

    for qi in range(q_ref.shape[1] // tq):
        q_tile(qi, 0)


def _fox_attn(q, k, v, cum, q_gain, k_gain):
    b, l, d = q.shape
    tq = min(ATTN_TILE, l)
    pairs = d // LANES
    return pl.pallas_call(
        _fox_attn_kernel,
        grid=(b, pairs),
        in_specs=[
            pl.BlockSpec((1, l, LANES), lambda i, p: (i, 0, p)),
            pl.BlockSpec((1, l, LANES), lambda i, p: (i, 0, p)),
            pl.BlockSpec((1, l, LANES), lambda i, p: (i, 0, p)),
            pl.BlockSpec((1, 1, 2, l), lambda i, p: (i, p, 0, 0)),
            pl.BlockSpec((1, LANES), lambda i, p: (0, 0)),
            pl.BlockSpec((1, LANES), lambda i, p: (0, 0)),
        ],
        out_specs=pl.BlockSpec((1, l, LANES), lambda i, p: (i, 0, p)),
        out_shape=jax.ShapeDtypeStruct((b, l, d), BF16),
        scratch_shapes=[
            pltpu.VMEM((l, LANES), BF16),
            pltpu.VMEM((l // tq, 2 * tq, tq), F32),
            pltpu.VMEM((2 * tq, LANES), F32),
            pltpu.VMEM((2 * tq, LANES), F32),
            pltpu.VMEM((2 * tq, LANES), F32),
        ],
        compiler_params=_params(("arbitrary", "arbitrary")),
        name="fox_attn",
    )(q, k, v, cum, q_gain, k_gain)


def _outproj_kernel(head_norm, o_ref, gate_ref, gain_ref, w_ref, x_ref, g1_ref, sc_ref, sh_ref,
                    wr_ref, br_ref, x1_ref, h2_ref, meta_ref, wt_ref, counts_ref):
    tm = x_ref.shape[1]

    @pl.when((pl.program_id(0) == 0) & (pl.program_id(1) == 0))
    def _():
        counts_ref[...] = jnp.zeros(counts_ref.shape, F32)

    o = o_ref[0].astype(F32)
    gate = gate_ref[0].astype(F32)
    if head_norm:
        parts = []
        for hh in range(N_HEADS_GDN):
            seg = o[:, hh * HEAD_DIM_GDN:(hh + 1) * HEAD_DIM_GDN]
            parts.append(seg * lax.rsqrt(jnp.mean(seg * seg, axis=-1, keepdims=True) + EPS))
        o = jnp.concatenate(parts, axis=1) * gain_ref[...]
    y = _dot((o * gate).astype(BF16), w_ref[...])
    x1 = x_ref[0] + g1_ref[0] * y
    x1_ref[0] = x1
    h2 = _rms_mod(x1, sc_ref[0], sh_ref[0])
    h2_ref[0] = _pack_bf16_pairs(h2)
    lg = _dot_split_weights(h2, wr_ref[...], LANES) + br_ref[...]
    row = lax.broadcasted_iota(jnp.int32, (tm, tm), 0)
    col = lax.broadcasted_iota(jnp.int32, (tm, tm), 1)
    meta, wt, counts = _route_tile(lg, (row > col).astype(BF16), counts_ref[...])
    meta_ref[0] = meta.T[:SUBLANES, :]
    wt_ref[0] = wt
    counts_ref[...] = counts


def _outproj(head_norm, o, gate, gain, w_out, x, g1, sc2, sh2, w_route, b_route):
    b, l, d = x.shape
    tm = min(ROW_TILE, l)
    row = lambda i, j: (i, j, 0)
    vec = lambda i, j: (i, 0, 0)
    const = lambda i, j: (0, 0)
    return pl.pallas_call(
        functools.partial(_outproj_kernel, head_norm),
        grid=(b, l // tm),
        in_specs=[
            pl.BlockSpec((1, tm, d), row),
            pl.BlockSpec((1, tm, d), row),
            pl.BlockSpec(gain.shape, const),
            pl.BlockSpec(w_out.shape, const),
            pl.BlockSpec((1, tm, d), row),
            pl.BlockSpec((1, 1, d), vec),
            pl.BlockSpec((1, 1, d), vec),
            pl.BlockSpec((1, 1, d), vec),
            pl.BlockSpec(w_route.shape, const),
            pl.BlockSpec(b_route.shape, const),
        ],
        out_specs=[pl.BlockSpec((1, tm, d), row), pl.BlockSpec((1, tm, d // 2), row),
                   pl.BlockSpec((1, SUBLANES, tm), lambda i, j: (i, 0, j)), pl.BlockSpec((1, tm, LANES), row),
                   pl.BlockSpec((1, LANES), const)],
        out_shape=[jax.ShapeDtypeStruct((b, l, d), F32), jax.ShapeDtypeStruct((b, l, d // 2), jnp.uint32),
                   jax.ShapeDtypeStruct((b, SUBLANES, l), jnp.int32), jax.ShapeDtypeStruct((b, l, LANES), F32),
                   jax.ShapeDtypeStruct((1, LANES), F32)],
        compiler_params=_params(("arbitrary", "arbitrary")),
        name="outproj",
    )(o, gate, gain, w_out, x, g1, sc2, sh2, w_route, b_route)


def _route_tile(lg, tri_strict, counts):
    lane_i = lax.broadcasted_iota(jnp.int32, lg.shape, 1)
    lane = lane_i.astype(F32)
    big = 1e9

    def first_argmax(v, vmax):
        return jnp.min(jnp.where(v == vmax, lane, big), axis=1, keepdims=True)

    is_group = lane < N_GROUPS
    gl = jnp.where(is_group, lg, NEG_BIG)
    g_max = jnp.max(gl, axis=1, keepdims=True)
    g_sum = jnp.sum(jnp.where(is_group, jnp.exp(gl - g_max), 0.0), axis=1, keepdims=True)
    g_p = 1.0 / g_sum
    g_idx = first_argmax(gl, g_max)
    lo = N_GROUPS + EXPERTS_PER_GROUP * g_idx
    el = jnp.where((lane >= lo) & (lane < lo + EXPERTS_PER_GROUP), lg, NEG_BIG)
    m1 = jnp.max(el, axis=1, keepdims=True)
    i1 = first_argmax(el, m1)
    el2 = jnp.where(lane == i1, NEG_BIG, el)
    m2 = jnp.max(el2, axis=1, keepdims=True)
    i2 = first_argmax(el2, m2)
    e = jnp.exp(m2 - m1)
    w1 = g_p / (1.0 + e)
    w2 = w1 * e
    e1 = i1 - N_GROUPS
    e2 = i2 - N_GROUPS
    hot1 = (lane == e1).astype(F32)
    hot2 = (lane == e2).astype(F32)
    hot = hot1 + hot2
    before = _dot(tri_strict, hot.astype(BF16)) + counts
    rank1 = jnp.sum(before * hot1, axis=1, keepdims=True)
    rank2 = jnp.sum(before * hot2, axis=1, keepdims=True)
    meta = jnp.where(lane_i == 0, e1, jnp.where(lane_i == 1, e2, jnp.where(lane_i == 2, rank1, rank2)))
    return (meta.astype(jnp.int32), jnp.where(lane_i == 0, w1, w2),
            counts + jnp.sum(hot, axis=0, keepdims=True))


def _index_windows(idx):
    n = idx.shape[0]
    return jnp.zeros((n // SC_WINDOW, LANES), jnp.int32).at[:, :SC_WINDOW].set(
        idx.reshape(n // SC_WINDOW, SC_WINDOW))


def _sc_mesh():
    return plsc.VectorSubcoreMesh(core_axis_name="core", subcore_axis_name="subcore")


def _scatter_rows(x, idx, n_rows):
    n_in, d = x.shape
    n_idx = idx.shape[0]

    def program(x_hbm, i_hbm, o_hbm):
        def window(x_vmem, i_vmem):
            pltpu.sync_copy(x_vmem, o_hbm.at[i_vmem.at[0, pl.ds(0, SC_WINDOW)]])

        pltpu.emit_pipeline(
            window,
            grid=(n_idx // SC_WINDOW,),
            in_specs=[pl.BlockSpec((SC_WINDOW, d), lambda i: (i % (n_in // SC_WINDOW), 0)),
                      pl.BlockSpec((1, LANES), lambda i: (i, 0))],
            out_specs=[],
            core_axis_name=("core", "subcore"),
            dimension_semantics=(pltpu.PARALLEL,),
        )(x_hbm, i_hbm)

    return pl.kernel(program, out_type=jax.ShapeDtypeStruct((n_rows, d), x.dtype), mesh=_sc_mesh(),
                     name="moe_scatter_rows")(x, _index_windows(idx))


def _gather_rows(x, idx):
    d = x.shape[1]
    n_idx = idx.shape[0]

    def program(x_hbm, i_hbm, o_hbm):
        def window(i_vmem, o_vmem):
            pltpu.sync_copy(x_hbm.at[i_vmem.at[0, pl.ds(0, SC_WINDOW)]], o_vmem)

        pltpu.emit_pipeline(
            window,
            grid=(n_idx // SC_WINDOW,),
            in_specs=[pl.BlockSpec((1, LANES), lambda i: (i, 0))],
            out_specs=[pl.BlockSpec((SC_WINDOW, d), lambda i: (i, 0))],
            core_axis_name=("core", "subcore"),
            dimension_semantics=(pltpu.PARALLEL,),
        )(i_hbm, o_hbm)

    return pl.kernel(program, out_type=jax.ShapeDtypeStruct((n_idx, d), x.dtype), mesh=_sc_mesh(),
                     name="moe_gather_rows")(x, _index_windows(idx))


def _gmm_kernel(first_ref, count_ref, xs_ref, wgu_ref, wd_ref, ys_ref, wgu_bf, wd_bf, xbuf, obuf, sem_in, sem_out):
    e = pl.program_id(0)
    last = pl.num_programs(0) - 1
    slots = xbuf.shape[0]
    tm = xbuf.shape[1]
    half = xbuf.shape[2]
    f = wd_bf.shape[0]
    n_tiles = count_ref[e]
    first = first_ref[e]
    total = first_ref[last] + count_ref[last]

    def rows(g):
        return pl.ds(pl.multiple_of(g * tm, tm), tm)

    def load(g):
        return pltpu.make_async_copy(xs_ref.at[rows(g)], xbuf.at[g % slots], sem_in.at[g % slots])

    def store(g):
        return pltpu.make_async_copy(obuf.at[g % slots], ys_ref.at[rows(g)], sem_out.at[g % slots])

    @pl.when(e == 0)
    def _():
        for g in range(slots - 1):
            @pl.when(g < total)
            def _():
                load(g).start()

    @pl.when(n_tiles > 0)
    def _():
        wgu_bf[...] = wgu_ref[0].astype(BF16)
        wd_bf[...] = wd_ref[0].astype(BF16)

    def tile(t, carry):
        g = first + t
        load(g).wait()

        @pl.when(g + slots - 1 < total)
        def _():
            load(g + slots - 1).start()

        @pl.when(g >= slots)
        def _():
            store(g - slots).wait()

        x_hi, x_lo = _unpack_bf16_pairs(xbuf[g % slots])
        hu = _dot(x_hi.astype(BF16), wgu_bf[:half, :]) + _dot(x_lo.astype(BF16), wgu_bf[half:, :])
        act = _silu(hu[:, :f]) * hu[:, f:]
        obuf[g % slots] = _pack_bf16_pairs(_dot(act.astype(BF16), wd_bf[...]))
        store(g).start()
        return carry

    lax.fori_loop(0, n_tiles, tile, 0)

    @pl.when(e == last)
    def _():
        for j in range(slots):
            g = total - slots + j

            @pl.when(g >= 0)
            def _():
                store(g).wait()


def _gmm(layer, tile_first, tile_count, xs, w_gate_up, w_down):
    n_rows, half = xs.shape
    d = 2 * half
    tm = MOE_TILE
    f2 = w_gate_up.shape[2]
    grid_spec = pltpu.PrefetchScalarGridSpec(
        num_scalar_prefetch=2,
        grid=(N_EXPERTS,),
        in_specs=[
            pl.BlockSpec(memory_space=pl.ANY),
            pl.BlockSpec((1, d, f2), lambda e, tf, tc: (layer * N_EXPERTS + e, 0, 0)),
            pl.BlockSpec((1, f2 // 2, d), lambda e, tf, tc: (layer * N_EXPERTS + e, 0, 0)),
        ],
        out_specs=pl.BlockSpec(memory_space=pl.ANY),
        scratch_shapes=[
            pltpu.VMEM((d, f2), BF16), pltpu.VMEM((f2 // 2, d), BF16),
            pltpu.VMEM((GMM_SLOTS, tm, half), jnp.uint32), pltpu.VMEM((GMM_SLOTS, tm, half), jnp.uint32),
            pltpu.SemaphoreType.DMA((GMM_SLOTS,)), pltpu.SemaphoreType.DMA((GMM_SLOTS,)),
        ],
    )
    return pl.pallas_call(
        _gmm_kernel,
        grid_spec=grid_spec,
        out_shape=jax.ShapeDtypeStruct((n_rows, half), jnp.uint32),
        input_output_aliases={2: 0},
        compiler_params=_params(("arbitrary",)),
        name="moe_gmm",
    )(tile_first, tile_count, xs, w_gate_up, w_down)


def _combine_kernel(y0_ref, y1_ref, wt_ref, x_ref, g2_ref, fg_ref, o_ref):
    x2 = x_ref[...] + g2_ref[0] * _combine_rows(y0_ref[...], y1_ref[...], wt_ref[...])
    o_ref[...] = x2 * lax.rsqrt(jnp.mean(x2 * x2, axis=-1, keepdims=True) + EPS) * fg_ref[...]


def _combine_final(yg, wt, x1, g2, final_gain, tokens_per_seq):
    t, d = x1.shape
    tt = min(TOKEN_TILE, tokens_per_seq)
    nb = t // tt
    per_seq = tokens_per_seq // tt
    return pl.pallas_call(
        _combine_kernel,
        grid=(nb,),
        in_specs=[
            pl.BlockSpec((tt, d // 2), lambda i: (i, 0)),
            pl.BlockSpec((tt, d // 2), lambda i: (i + nb, 0)),
            pl.BlockSpec((tt, LANES), lambda i: (i, 0)),
            pl.BlockSpec((tt, d), lambda i: (i, 0)),
            pl.BlockSpec((1, 1, d), lambda i: (i // per_seq, 0, 0)),
            pl.BlockSpec((1, d), lambda i: (0, 0)),
        ],
        out_specs=pl.BlockSpec((tt, d), lambda i: (i, 0)),
        out_shape=jax.ShapeDtypeStruct((t, d), F32),
        compiler_params=_params(("arbitrary",)),
        name="moe_combine",
    )(yg, yg, wt, x1, g2, final_gain)


def _sorted_positions(meta, counts, tile):
    counts = counts[0, :N_EXPERTS].astype(jnp.int32)
    tiles_per = (counts + tile - 1) // tile
    tile_first = jnp.cumsum(tiles_per) - tiles_per
    choice_major = lambda a: jnp.transpose(a, (1, 0, 2)).reshape(TOP_K, -1)
    eid = choice_major(meta[:, :TOP_K, :])
    rank = choice_major(meta[:, TOP_K:2 * TOP_K, :])
    onehot = (eid[..., None] == jnp.arange(N_EXPERTS, dtype=jnp.int32)).astype(jnp.int32)
    pos = jnp.sum(onehot * (tile_first * tile), axis=-1) + rank
    return pos.astype(jnp.int32), tile_first.astype(jnp.int32), tiles_per.astype(jnp.int32)


def _sorted_rows(n_tokens, tile):
    return (TOP_K * n_tokens // tile + N_EXPERTS) * tile


def _hier_moe(layer, h2, meta, counts, w_gate_up, w_down):
    pos, tile_first, tile_count = _sorted_positions(meta, counts, MOE_TILE)
    pos = pos.reshape(-1)
    xs = _scatter_rows(h2, pos, _sorted_rows(h2.shape[0], MOE_TILE))
    ys = _gmm(layer, tile_first, tile_count, xs, w_gate_up, w_down)
    return _gather_rows(ys, pos)


def _hi_lo_cols(w, n):
    k, m = w.shape
    wp = jnp.zeros((k, n), F32).at[:, :m].set(w)
    hi = wp.astype(BF16)
    lo = (wp - hi.astype(F32)).astype(BF16)
    return jnp.concatenate([hi, lo], axis=1)


def _pad_row(v, n):
    return jnp.zeros((1, n), F32).at[0, :v.shape[0]].set(v)


def kernel(x, c, ada_w, ada_b, gdn_w_in, gdn_conv, gdn_a_log, gdn_dt_bias, gdn_norm, gdn_w_out,
           fox_w_in, fox_b_f, fox_q_norm, fox_k_norm, fox_w_out,
           moe_w_group, moe_b_group, moe_w_router, moe_b_router, moe_w_gate_up, moe_w_down,
           final_norm):
    b, l, d = x.shape
    depth = ada_w.shape[0]
    t = b * l
    ada = _ada(c, ada_w, ada_b)
    w_gate_up = moe_w_gate_up.reshape(depth * N_EXPERTS, d, moe_w_gate_up.shape[-1])
    w_down = moe_w_down.reshape(depth * N_EXPERTS, moe_w_down.shape[-2], d)
    final_gain = final_norm.reshape(1, d)

    x_src = x
    for i in range(depth):
        sh1, sc1, g1, sh2, sc2, g2 = [ada[i, :, s * d:(s + 1) * d].reshape(b, 1, d) for s in range(6)]
        j = i // 2
        w_route = _hi_lo_cols(jnp.concatenate([moe_w_group[i], moe_w_router[i]], axis=1), LANES)
        b_route = _pad_row(jnp.concatenate([moe_b_group[i], moe_b_router[i]]), LANES)
        if i % 2 == 0:
            w_in = gdn_w_in[j]
            q, k, v, gate, gb, *x_new = _gdn_inproj(
                x_src, sc1, sh1, w_in.astype(BF16), _hi_lo_cols(w_in[:, 4 * d:], LANES),
                gdn_conv[j], jnp.concatenate([_pad_row(gdn_a_log[j], LANES), _pad_row(gdn_dt_bias[j], LANES)]))
            x = x_new[0] if x_new else x_src
            o = _gdn_chunk(q, k, v, gb)
            gain = jnp.tile(gdn_norm[j], N_HEADS_GDN).reshape(1, d)
            x1, h2, meta, wt, counts = _outproj(True, o, gate, gain, gdn_w_out[j].astype(BF16), x, g1, sc2, sh2,
                                      w_route, b_route)
        else:
            w_in = fox_w_in[j]
            q, k, v, gate, cum_t, *x_new = _fox_inproj(
                x_src, sc1, sh1, w_in.astype(BF16), _hi_lo_cols(w_in[:, 4 * d:], LANES),
                _pad_row(fox_b_f[j], LANES))
            x = x_new[0] if x_new else x_src
            cum =cum_t[:, :N_HEADS_FOX, :].reshape(b, N_HEADS_FOX // 2, 2, l)
            q_gain = jnp.tile(fox_q_norm[j], 2).reshape(1, LANES)
            k_gain = jnp.tile(fox_k_norm[j], 2).reshape(1, LANES)
            o = _fox_attn(q, k, v, cum, q_gain, k_gain)
            x1, h2, meta, wt, counts = _outproj(False, o, gate, final_gain, fox_w_out[j].astype(BF16), x, g1, sc2, sh2,
                                      w_route, b_route)
        yg = _hier_moe(i, h2.reshape(t, d // 2), meta, counts, w_gate_up, w_down)
        x_src = (x1, yg, wt.reshape(t, LANES), g2)
    x1, yg, wt, g2 = x_src
    return _combine_final(yg, wt, x1.reshape(t, d), g2, final_gain, l).reshape(b, l, d)
```

```python
import functools

import jax
import jax.numpy as jnp
from jax import lax
from jax.experimental import pallas as pl
from jax.experimental.pallas import tpu as pltpu
from jax.experimental.pallas import tpu_sc as plsc

F32 = jnp.float32
BF16 = jnp.bfloat16
PACKED = jnp.int32

EPS = 1e-6
CHUNK = 64
N_HEADS_GDN = 8
HEAD_DIM_GDN = 128
N_HEADS_FOX = 16
HEAD_DIM_FOX = 64
CONV_K = 4
N_GROUPS = 4
EXPERTS_PER_GROUP = 8
N_EXPERTS = N_GROUPS * EXPERTS_PER_GROUP
TOP_K = 2

LANES = 128
SUBLANES = 8
VMEM_LIMIT_BYTES = 56 * 1024 * 1024

ROW_TILE = 512
ATTN_TILE = 512
ATTN_UNROLL = 2
GDN_BLOCK = 512
GDN_CHUNKS_PER_ITER = 4
GDN_HEADS_PACKED = 4
MOE_TILE = 256
GMM_SLOTS = 4
TOKEN_TILE = 512
SC_WINDOW = 64
NEG_BIG = -1e30
LOG2E = 1.4426950408889634


def _params(semantics):
    return pltpu.CompilerParams(dimension_semantics=semantics, vmem_limit_bytes=VMEM_LIMIT_BYTES)


def _sigmoid(x):
    return 0.5 * jnp.tanh(0.5 * x) + 0.5


def _silu(x):
    return x * _sigmoid(x)


def _softplus(x):
    return jnp.maximum(x, 0.0) + jnp.log(1.0 + jnp.exp(-jnp.abs(x)))


def _split2(a):
    hi = a.astype(BF16)
    lo = (a - hi.astype(F32)).astype(BF16)
    return hi, lo


def _split3(a):
    hi = a.astype(BF16)
    r = a - hi.astype(F32)
    mid = r.astype(BF16)
    lo = (r - mid.astype(F32)).astype(BF16)
    return hi, mid, lo


def _dot(a, b):
    return jnp.dot(a, b, preferred_element_type=F32)


def _dot_nt(a, b):
    return lax.dot_general(a, b, (((1,), (1,)), ((), ())), preferred_element_type=F32)


def _dot_split_weights(a, w_hi_lo, n):
    a_hi, a_lo = _split2(a)
    r = _dot(a_hi, w_hi_lo)
    return r[:, :n] + r[:, n:] + _dot(a_lo, w_hi_lo[:, :n])


def _cumsum_rows(tri_bf16, v):
    hi, mid, lo = _split3(v)
    r = _dot(tri_bf16, jnp.concatenate([hi, mid], axis=1))
    return r[:, :LANES] + r[:, LANES:] + _dot(tri_bf16, lo)


def _rms_mod(x, sc, sh):
    ms = jnp.mean(x * x, axis=-1, keepdims=True)
    return x * lax.rsqrt(ms + EPS) * (1.0 + sc) + sh


def _pack_bf16_pairs(a):
    n = a.shape[1] // 2
    bits = lax.bitcast_convert_type(a.astype(BF16).astype(F32), jnp.uint32)
    words = (bits[:, :n] & jnp.uint32(0xFFFF0000)) | (bits[:, n:] >> 16)
    return lax.bitcast_convert_type(words, PACKED)


def _unpack_bf16_pairs(p):
    words = lax.bitcast_convert_type(p, jnp.uint32)
    hi = lax.bitcast_convert_type(words & jnp.uint32(0xFFFF0000), F32)
    lo = lax.bitcast_convert_type(words << 16, F32)
    return hi, lo


def _lane_col(v, idx, lane):
    return jnp.sum(jnp.where(lane == idx, v, 0.0), axis=1, keepdims=True)


def _ada_kernel(c_ref, w_ref, b_ref, o_ref):
    c = c_ref[...]
    o_ref[0] = _dot(_silu(c).astype(BF16), w_ref[0].astype(BF16)) + b_ref[0]


def _ada(c, ada_w, ada_b):
    depth, d, n = ada_w.shape
    b = c.shape[0]
    tn = 1536
    return pl.pallas_call(
        _ada_kernel,
        grid=(depth, n // tn),
        in_specs=[
            pl.BlockSpec((b, d), lambda i, j: (0, 0)),
            pl.BlockSpec((1, d, tn), lambda i, j: (i, 0, j)),
            pl.BlockSpec((1, 1, tn), lambda i, j: (i, 0, j)),
        ],
        out_specs=pl.BlockSpec((1, b, tn), lambda i, j: (i, 0, j)),
        out_shape=jax.ShapeDtypeStruct((depth, b, n), F32),
        compiler_params=_params(("arbitrary", "arbitrary")),
        name="ada",
    )(c, ada_w, ada_b.reshape(depth, 1, n))


def _combine_rows(y0, y1, wt):
    lane = lax.broadcasted_iota(jnp.int32, wt.shape, 1)
    w0 = _lane_col(wt, 0, lane)
    w1 = _lane_col(wt, 1, lane)
    a_hi, a_lo = _unpack_bf16_pairs(y0)
    b_hi, b_lo = _unpack_bf16_pairs(y1)
    return jnp.concatenate([w0 * a_hi + w1 * b_hi, w0 * a_lo + w1 * b_lo], axis=1)


N_PENDING_REFS = 5


def _residual_tile(pending, refs):
    if not pending:
        return refs[0][0]
    x1_ref, y0_ref, y1_ref, wt_ref, g2_ref = refs
    return x1_ref[0] + g2_ref[0] * _combine_rows(y0_ref[...], y1_ref[...], wt_ref[...])


def _gdn_inproj_kernel(pending, *refs):
    nx = N_PENDING_REFS if pending else 1
    sc_ref, sh_ref, w_ref, wab_ref, conv_ref, prm_ref, q_ref, k_ref, v_ref, gate_ref, gb_ref = refs[nx:nx + 11]
    halo_ref = refs[-1]
    x = _residual_tile(pending, refs[:nx])
    if pending:
        refs[nx + 11][0] = x
    tm, d = x.shape
    h = _rms_mod(x, sc_ref[0], sh_ref[0])
    hb = h.astype(BF16)

    @pl.when(pl.program_id(1) == 0)
    def _():
        halo_ref[:, 0:SUBLANES, :] = jnp.zeros((3, SUBLANES, d), F32)

    for s, o_ref in enumerate((q_ref, k_ref, v_ref)):
        raw = _dot(hb, w_ref[:, s * d:(s + 1) * d])
        halo_ref[s, SUBLANES:SUBLANES + tm, :] = raw
        cw = conv_ref[:, s * d:(s + 1) * d]
        y = raw * cw[3:4]
        for j in range(CONV_K - 1):
            off = SUBLANES - (CONV_K - 1) + j
            y = y + halo_ref[s, off:off + tm, :] * cw[j:j + 1]
        halo_ref[s, 0:SUBLANES, :] = halo_ref[s, tm:tm + SUBLANES, :]
        y = _silu(y)
        if s < 2:
            scale = HEAD_DIM_GDN ** -0.5 if s == 0 else 1.0
            for hh in range(N_HEADS_GDN):
                seg = y[:, hh * HEAD_DIM_GDN:(hh + 1) * HEAD_DIM_GDN]
                inv = lax.rsqrt(jnp.sum(seg * seg, axis=-1, keepdims=True) + EPS) * scale
                o_ref[0, :, hh * HEAD_DIM_GDN:(hh + 1) * HEAD_DIM_GDN] = (seg * inv).astype(o_ref.dtype)
        else:
            o_ref[0] = y.astype(o_ref.dtype)

    gate_ref[0] = _silu(_dot(hb, w_ref[:, 3 * d:4 * d])).astype(gate_ref.dtype)

    ab = _dot_split_weights(h, wab_ref[...], LANES)
    log_alpha = -jnp.exp(prm_ref[0:1, :]) * _softplus(ab + prm_ref[1:2, :])
    lane = lax.broadcasted_iota(jnp.int32, ab.shape, 1)
    gb_ref[0] = jnp.where(lane < N_HEADS_GDN, log_alpha, _sigmoid(ab))


def _residual_operands(x_src, tm):
    row = lambda i, j: (i, j, 0)
    if not isinstance(x_src, tuple):
        b, l, d = x_src.shape
        return False, [x_src], [pl.BlockSpec((1, tm, d), row)], [], []
    x1, yg, wt, g2 = x_src
    b, l, d = x1.shape
    per_seq = l // tm
    n_tiles = b * per_seq
    specs = [
        pl.BlockSpec((1, tm, d), row),
        pl.BlockSpec((tm, d // 2), lambda i, j: (i * per_seq + j, 0)),
        pl.BlockSpec((tm, d // 2), lambda i, j: (i * per_seq + j + n_tiles, 0)),
        pl.BlockSpec((tm, LANES), lambda i, j: (i * per_seq + j, 0)),
        pl.BlockSpec((1, 1, d), lambda i, j: (i, 0, 0)),
    ]
    return (True, [x1, yg, yg, wt, g2], specs, [jax.ShapeDtypeStruct((b, l, d), F32)],
            [pl.BlockSpec((1, tm, d), row)])


def _gdn_inproj(x_src, sc, sh, w_main, w_ab, conv_w, prm):
    b, l, d = (x_src[0] if isinstance(x_src, tuple) else x_src).shape
    tm = min(ROW_TILE, l)
    pending, x_ops, x_specs, x_out_shape, x_out_spec = _residual_operands(x_src, tm)
    act = jax.ShapeDtypeStruct((b, l, d), BF16)
    row = lambda i, j: (i, j, 0)
    vec = lambda i, j: (i, 0, 0)
    const = lambda i, j: (0, 0)
    return pl.pallas_call(
        functools.partial(_gdn_inproj_kernel, pending),
        grid=(b, l // tm),
        in_specs=x_specs + [
            pl.BlockSpec((1, 1, d), vec),
            pl.BlockSpec((1, 1, d), vec),
            pl.BlockSpec(w_main.shape, const),
            pl.BlockSpec(w_ab.shape, const),
            pl.BlockSpec(conv_w.shape, const),
            pl.BlockSpec(prm.shape, const),
        ],
        out_specs=[pl.BlockSpec((1, tm, d), row)] * 4 + [pl.BlockSpec((1, tm, LANES), row)] + x_out_spec,
        out_shape=[act, act, act, act, jax.ShapeDtypeStruct((b, l, LANES), F32)] + x_out_shape,
        scratch_shapes=[pltpu.VMEM((3, tm + SUBLANES, d), F32)],
        compiler_params=_params(("arbitrary", "arbitrary")),
        name="gdn_inproj",
    )(*x_ops, sc, sh, w_main, w_ab, conv_w, prm)


def _block_diag(r, blk, n_blk):
    return jnp.concatenate([jnp.where(blk == b, r, jnp.zeros_like(r)) for b in range(n_blk)], axis=0)


def _mm_packed(lhs, rhs, blk, n_blk):
    m = lhs.shape[0]
    l_hi, l_lo = _split2(lhs)
    r_hi, r_lo = _split2(rhs)
    t = _dot(jnp.concatenate([l_hi, l_lo], axis=0), _block_diag(r_hi, blk, n_blk))
    return t[:m] + t[m:] + _dot(l_hi, _block_diag(r_lo, blk, n_blk))


def _unit_lower_inverse(a_strict, eye, blk, n_blk):
    n = a_strict[0].shape[0]
    levels = n.bit_length() - 1
    m = [-a for a in a_strict]
    x = [eye + mi for mi in m]
    p = [_mm_packed(mi, mi, blk, n_blk) for mi in m]
    for _ in range(levels - 2):
        r = [_mm_packed(jnp.concatenate([pi, xi], axis=0), pi, blk, n_blk) for pi, xi in zip(p, x)]
        x = [xi + ri[n:] for xi, ri in zip(x, r)]
        p = [ri[:n] for ri in r]
    return [xi + _mm_packed(xi, pi, blk, n_blk) for xi, pi in zip(x, p)]


def _gdn_chunk_kernel(q_ref, k_ref, v_ref, gb_ref, o_ref, s_ref):
    lb = q_ref.shape[1]
    c = CHUNK
    dk = HEAD_DIM_GDN
    nc = GDN_CHUNKS_PER_ITER
    heads = range(N_HEADS_GDN)
    sl = [slice(hh * dk, (hh + 1) * dk) for hh in heads]

    @pl.when(pl.program_id(1) == 0)
    def _():
        s_ref[...] = jnp.zeros(s_ref.shape, F32)

    pk = GDN_HEADS_PACKED
    groups = N_HEADS_GDN // pk
    tri = (lax.broadcasted_iota(jnp.int32, (c, c), 0) >= lax.broadcasted_iota(jnp.int32, (c, c), 1)).astype(BF16)
    lane = lax.broadcasted_iota(jnp.int32, (c, LANES), 1)
    side = lane // c
    row_p = lax.broadcasted_iota(jnp.int32, (c, pk * c), 0)
    lane_p = lax.broadcasted_iota(jnp.int32, (c, pk * c), 1)
    blk = lane_p // c
    col_p = lane_p - blk * c
    incl = row_p >= col_p
    strict = row_p > col_p
    diag = row_p == col_p
    eye = diag.astype(F32)
    blk_k = lax.broadcasted_iota(jnp.int32, (c, pk * dk), 1) // dk

    def per_head(cols, width):
        return jnp.concatenate([jnp.broadcast_to(col, (c, width)) for col in cols], axis=1)

    def body(it, carry):
        r0, q, kd_t, egl_last, rhs_pair, kq, decay, eg_cols = [], [], [], [], [], [], [], []
        for j in range(nc):
            r = pl.multiple_of((it * nc + j) * c, c)
            r0.append(r)
            gb = gb_ref[0, pl.ds(r, c), :]
            g = _cumsum_rows(tri, gb)
            g_last = g[c - 1:c, :]
            eg = jnp.exp(g)
            egl = jnp.exp(g_last - g)
            eg_last = jnp.exp(g_last)
            for gi in range(groups):
                hs = [gi * pk + b for b in range(pk)]
                cols = slice(gi * pk * dk, (gi + 1) * pk * dk)
                q4 = q_ref[0, pl.ds(r, c), cols].astype(F32)
                k4 = k_ref[0, pl.ds(r, c), cols].astype(F32)
                v4 = v_ref[0, pl.ds(r, c), cols].astype(F32)
                beta4 = per_head([_lane_col(gb, N_HEADS_GDN + hh, lane) for hh in hs], dk)
                eg4 = per_head([_lane_col(eg, hh, lane) for hh in hs], dk)
                egl4 = per_head([_lane_col(egl, hh, lane) for hh in hs], dk)
                kb4 = k4 * beta4
                k4b = k4.astype(BF16)
                k_diag = jnp.concatenate([jnp.where(blk_k == b, k4b, jnp.zeros_like(k4b)) for b in range(pk)],
                                         axis=0)
                kq.append(_dot_nt(jnp.concatenate([kb4, q4], axis=0).astype(BF16), k_diag))
                g_col = per_head([_lane_col(g, hh, lane) for hh in hs], c)
                g_row = jnp.sum(jnp.where(diag, g_col, 0.0), axis=0, keepdims=True)
                decay.append(jnp.where(incl, jnp.exp(jnp.where(incl, g_col - g_row, 0.0)), 0.0))
                vb4 = v4 * beta4
                kbe4 = kb4 * eg4
                qe4 = q4 * eg4
                kd4 = k4 * egl4
                for b, hh in enumerate(hs):
                    hsl = slice(b * dk, (b + 1) * dk)
                    q.append(qe4[:, hsl])
                    kd_t.append(kd4[:, hsl].T.astype(BF16))
                    egl_last.append(_lane_col(eg_last, hh, lane[0:1]))
                for p in range(pk // 2):
                    a = slice(2 * p * dk, (2 * p + 1) * dk)
                    bsl = slice((2 * p + 1) * dk, (2 * p + 2) * dk)
                    rhs_pair.append(jnp.concatenate(
                        [jnp.concatenate([vb4[:, a], kbe4[:, a]], axis=1),
                         jnp.concatenate([vb4[:, bsl], kbe4[:, bsl]], axis=1)], axis=0).astype(BF16))
        n_grp = len(kq)
        t_inv = _unit_lower_inverse([jnp.where(strict, kq[i][:c] * decay[i], 0.0) for i in range(n_grp)],
                                    eye, blk, pk)
        qk_p = [kq[i][c:] * decay[i] for i in range(n_grp)]

        def head_lhs(packed, i_grp, b):
            tile = packed[i_grp][:, (b // 2) * LANES:(b // 2 + 1) * LANES]
            return jnp.where(side == b % 2, tile, 0.0).astype(BF16)

        uw, qk = [], []
        for i_grp in range(n_grp):
            for b in range(pk):
                uw.append(_dot(head_lhs(t_inv, i_grp, b), rhs_pair[i_grp * (pk // 2) + b // 2]))
                qk.append(head_lhs(qk_p, i_grp, b))
        n = len(uw)
        wq = [jnp.concatenate([uw[i][:, dk:], q[i]], axis=0).astype(BF16) for i in range(n)]

        s = [s_ref[hh] for hh in heads]
        for j in range(nc):
            idx = [j * N_HEADS_GDN + hh for hh in heads]
            ws_qs = [_dot(wq[i], s[hh].astype(BF16)) for hh, i in zip(heads, idx)]
            v_new = [(uw[i][:, :dk] - ws_qs[hh][:c]).astype(BF16) for hh, i in zip(heads, idx)]
            for hh, i in zip(heads, idx):
                pair = jnp.concatenate([v_new[hh - hh % 2], v_new[hh - hh % 2 + 1]], axis=0)
                o = ws_qs[hh][c:] + _dot(qk[i], pair)
                o_ref[0, pl.ds(r0[j], c), sl[hh]] = o.astype(o_ref.dtype)
            s = [s[hh] * egl_last[i] + _dot(kd_t[i], v_new[hh]) for hh, i in zip(heads, idx)]
        for hh in heads:
            s_ref[hh] = s[hh]
        return carry

    lax.fori_loop(0, lb // (c * nc), body, 0)


def _gdn_chunk(q, k, v, gb):
    b, l, d = q.shape
    lb = min(GDN_BLOCK, l)
    row = lambda i, j: (i, j, 0)
    return pl.pallas_call(
        _gdn_chunk_kernel,
        grid=(b, l // lb),
        in_specs=[pl.BlockSpec((1, lb, d), row)] * 3 + [pl.BlockSpec((1, lb, LANES), row)],
        out_specs=pl.BlockSpec((1, lb, d), row),
        out_shape=jax.ShapeDtypeStruct((b, l, d), F32),
        scratch_shapes=[pltpu.VMEM((N_HEADS_GDN, HEAD_DIM_GDN, HEAD_DIM_GDN), F32)],
        compiler_params=_params(("arbitrary", "arbitrary")),
        name="gdn_chunk",
    )(q, k, v, gb)


def _fox_inproj_kernel(pending, *refs):
    nx = N_PENDING_REFS if pending else 1
    sc_ref, sh_ref, w_ref, wf_ref, bf_ref, q_ref, k_ref, v_ref, g_ref, cum_t_ref = refs[nx:nx + 10]
    carry_ref = refs[-1]
    x = _residual_tile(pending, refs[:nx])
    if pending:
        refs[nx + 10][0] = x
    tm, d = x.shape
    h = _rms_mod(x, sc_ref[0], sh_ref[0])
    hb = h.astype(BF16)
    for s, o_ref in enumerate((q_ref, k_ref, v_ref)):
        o_ref[0] = _dot(hb, w_ref[:, s * d:(s + 1) * d]).astype(o_ref.dtype)
    g_ref[0] = _sigmoid(_dot(hb, w_ref[:, 3 * d:4 * d])).astype(g_ref.dtype)

    @pl.when(pl.program_id(1) == 0)
    def _():
        carry_ref[...] = jnp.zeros(carry_ref.shape, F32)

    f_logit = _dot_split_weights(h, wf_ref[...], LANES) + bf_ref[...]
    log_f = -_softplus(-f_logit)
    row = lax.broadcasted_iota(jnp.int32, (tm, tm), 0)
    col = lax.broadcasted_iota(jnp.int32, (tm, tm), 1)
    cum = _cumsum_rows((row >= col).astype(BF16), log_f) + carry_ref[...]
    carry_ref[...] = cum[tm - 1:tm, :]
    cum_t_ref[0] = cum.T


def _fox_inproj(x_src, sc, sh, w_main, w_f, b_f):
    b, l, d = (x_src[0] if isinstance(x_src, tuple) else x_src).shape
    tm = min(ROW_TILE, l)
    pending, x_ops, x_specs, x_out_shape, x_out_spec = _residual_operands(x_src, tm)
    act = jax.ShapeDtypeStruct((b, l, d), BF16)
    row = lambda i, j: (i, j, 0)
    vec = lambda i, j: (i, 0, 0)
    const = lambda i, j: (0, 0)
    return pl.pallas_call(
        functools.partial(_fox_inproj_kernel, pending),
        grid=(b, l // tm),
        in_specs=x_specs + [
            pl.BlockSpec((1, 1, d), vec),
            pl.BlockSpec((1, 1, d), vec),
            pl.BlockSpec(w_main.shape, const),
            pl.BlockSpec(w_f.shape, const),
            pl.BlockSpec(b_f.shape, const),
        ],
        out_specs=([pl.BlockSpec((1, tm, d), row)] * 4 + [pl.BlockSpec((1, LANES, tm), lambda i, j: (i, 0, j))]
                   + x_out_spec),
        out_shape=[act, act, act, act, jax.ShapeDtypeStruct((b, LANES, l), F32)] + x_out_shape,
        scratch_shapes=[pltpu.VMEM((1, LANES), F32)],
        compiler_params=_params(("arbitrary", "arbitrary")),
        name="fox_inproj",
    )(*x_ops, sc, sh, w_main, w_f, b_f)


def _fox_attn_kernel(q_ref, k_ref, v_ref, cum_ref, qg_ref, kg_ref, o_ref, kn_ref, s_ref, mx_ref, ls_ref, acc_ref):
    tq = s_ref.shape[2]
    hd = HEAD_DIM_FOX
    lane = lax.broadcasted_iota(jnp.int32, (1, LANES), 1)
    first = lane < hd
    n_col = tq // LANES

    def head_norm(t, gain):
        sq = t * t
        s0 = jnp.sum(jnp.where(first, sq, 0.0), axis=1, keepdims=True)
        s1 = jnp.sum(jnp.where(first, 0.0, sq), axis=1, keepdims=True)
        ms = jnp.where(first, s0, s1) * (1.0 / hd)
        return t * lax.rsqrt(ms + EPS) * gain

    kn_ref[...] = head_norm(k_ref[0].astype(F32), kg_ref[...]).astype(kn_ref.dtype)

    def q_tile(qi, carry):
        q0 = qi * tq
        qn = head_norm(q_ref[0, pl.ds(q0, tq), :].astype(F32), qg_ref[...]) * (hd ** -0.5 * LOG2E)
        q2 = jnp.concatenate([jnp.where(first, qn, 0.0), jnp.where(first, 0.0, qn)], axis=0).astype(BF16)

        def scores(j, masked):
            k0 = j * tq if isinstance(j, int) else pl.multiple_of(j * tq, tq)
            ck = cum_ref[0, 0, :, pl.ds(k0, tq)] * LOG2E
            s = _dot_nt(q2, kn_ref[pl.ds(k0, tq), :])
            s = jnp.concatenate([s[:tq] - ck[0:1], s[tq:] - ck[1:2]], axis=0)
            if masked:
                r = lax.broadcasted_iota(jnp.int32, (tq, tq), 0)
                c = lax.broadcasted_iota(jnp.int32, (tq, tq), 1)
                keep = jnp.concatenate([r >= c, r >= c], axis=0)
                s = jnp.where(keep, s, NEG_BIG)
            s_ref[j] = s
            mx = mx_ref[...]
            for t in range(n_col):
                mx = jnp.maximum(mx, s[:, t * LANES:(t + 1) * LANES])
            mx_ref[...] = mx

        def scores_step(j, c):
            scores(j, False)
            return c

        mx_ref[...] = jnp.full(mx_ref.shape, NEG_BIG, F32)
        if qi > 0:
            lax.fori_loop(0, qi, scores_step, 0, unroll=min(qi, ATTN_UNROLL))
        scores(qi, True)
        mx_ref[...] = jnp.broadcast_to(jnp.max(mx_ref[...], axis=1, keepdims=True), mx_ref.shape)

        acc_ref[...] = jnp.zeros(acc_ref.shape, F32)
        ls_ref[...] = jnp.zeros(ls_ref.shape, F32)

        def weighted_sum(j, c):
            k0 = j * tq if isinstance(j, int) else pl.multiple_of(j * tq, tq)
            m = mx_ref[...]
            s = s_ref[j]
            p = jnp.concatenate([jnp.exp2(s[:, t * LANES:(t + 1) * LANES] - m) for t in range(n_col)], axis=1)
            acc_ref[...] += _dot(p.astype(BF16), v_ref[0, pl.ds(k0, tq), :])
            ls = ls_ref[...]
            for t in range(n_col):
                ls = ls + p[:, t * LANES:(t + 1) * LANES]
            ls_ref[...] = ls
            return c

        lax.fori_loop(0, qi + 1, weighted_sum, 0, unroll=min(qi + 1, ATTN_UNROLL))

        out = acc_ref[...] / jnp.sum(ls_ref[...], axis=1, keepdims=True)
        o_ref[0, pl.ds(q0, tq), :] = jnp.where(first, out[:tq], out[tq:]).astype(o_ref.dtype)
        return carry

    for qi in range(q_ref.shape[1] // tq):
        q_tile(qi, 0)


def _fox_attn(q, k, v, cum, q_gain, k_gain):
    b, l, d = q.shape
    tq = min(ATTN_TILE, l)
    pairs = d // LANES
    return pl.pallas_call(
        _fox_attn_kernel,
        grid=(b, pairs),
        in_specs=[
            pl.BlockSpec((1, l, LANES), lambda i, p: (i, 0, p)),
            pl.BlockSpec((1, l, LANES), lambda i, p: (i, 0, p)),
            pl.BlockSpec((1, l, LANES), lambda i, p: (i, 0, p)),
            pl.BlockSpec((1, 1, 2, l), lambda i, p: (i, p, 0, 0)),
            pl.BlockSpec((1, LANES), lambda i, p: (0, 0)),
            pl.BlockSpec((1, LANES), lambda i, p: (0, 0)),
        ],
        out_specs=pl.BlockSpec((1, l, LANES), lambda i, p: (i, 0, p)),
        out_shape=jax.ShapeDtypeStruct((b, l, d), BF16),
        scratch_shapes=[
            pltpu.VMEM((l, LANES), BF16),
            pltpu.VMEM((l // tq, 2 * tq, tq), F32),
            pltpu.VMEM((2 * tq, LANES), F32),
            pltpu.VMEM((2 * tq, LANES), F32),
            pltpu.VMEM((2 * tq, LANES), F32),
        ],
        compiler_params=_params(("arbitrary", "arbitrary")),
        name="fox_attn",
    )(q, k, v, cum, q_gain, k_gain)


def _outproj_kernel(head_norm, o_ref, gate_ref, gain_ref, w_ref, x_ref, g1_ref, sc_ref, sh_ref,
                    wr_ref, br_ref, x1_ref, h2_ref, meta_ref, wt_ref, counts_ref):
    tm = x_ref.shape[1]

    @pl.when((pl.program_id(0) == 0) & (pl.program_id(1) == 0))
    def _():
        counts_ref[...] = jnp.zeros(counts_ref.shape, F32)

    o = o_ref[0].astype(F32)
    gate = gate_ref[0].astype(F32)
    if head_norm:
        parts = []
        for hh in range(N_HEADS_GDN):
            seg = o[:, hh * HEAD_DIM_GDN:(hh + 1) * HEAD_DIM_GDN]
            parts.append(seg * lax.rsqrt(jnp.mean(seg * seg, axis=-1, keepdims=True) + EPS))
        o = jnp.concatenate(parts, axis=1) * gain_ref[...]
    y = _dot((o * gate).astype(BF16), w_ref[...])
    x1 = x_ref[0] + g1_ref[0] * y
    x1_ref[0] = x1
    h2 = _rms_mod(x1, sc_ref[0], sh_ref[0])
    h2_ref[0] = _pack_bf16_pairs(h2)
    lg = _dot_split_weights(h2, wr_ref[...], LANES) + br_ref[...]
    row = lax.broadcasted_iota(jnp.int32, (tm, tm), 0)
    col = lax.broadcasted_iota(jnp.int32, (tm, tm), 1)
    meta, wt, counts = _route_tile(lg, (row > col).astype(BF16), counts_ref[...])
    meta_ref[0] = meta.T[:SUBLANES, :]
    wt_ref[0] = wt
    counts_ref[...] = counts


def _outproj(head_norm, o, gate, gain, w_out, x, g1, sc2, sh2, w_route, b_route):
    b, l, d = x.shape
    tm = min(ROW_TILE, l)
    row = lambda i, j: (i, j, 0)
    vec = lambda i, j: (i, 0, 0)
    const = lambda i, j: (0, 0)
    return pl.pallas_call(
        functools.partial(_outproj_kernel, head_norm),
        grid=(b, l // tm),
        in_specs=[
            pl.BlockSpec((1, tm, d), row),
            pl.BlockSpec((1, tm, d), row),
            pl.BlockSpec(gain.shape, const),
            pl.BlockSpec(w_out.shape, const),
            pl.BlockSpec((1, tm, d), row),
            pl.BlockSpec((1, 1, d), vec),
            pl.BlockSpec((1, 1, d), vec),
            pl.BlockSpec((1, 1, d), vec),
            pl.BlockSpec(w_route.shape, const),
            pl.BlockSpec(b_route.shape, const),
        ],
        out_specs=[pl.BlockSpec((1, tm, d), row), pl.BlockSpec((1, tm, d // 2), row),
                   pl.BlockSpec((1, SUBLANES, tm), lambda i, j: (i, 0, j)), pl.BlockSpec((1, tm, LANES), row),
                   pl.BlockSpec((1, LANES), const)],
        out_shape=[jax.ShapeDtypeStruct((b, l, d), F32), jax.ShapeDtypeStruct((b, l, d // 2), PACKED),
                   jax.ShapeDtypeStruct((b, SUBLANES, l), jnp.int32), jax.ShapeDtypeStruct((b, l, LANES), F32),
                   jax.ShapeDtypeStruct((1, LANES), F32)],
        compiler_params=_params(("arbitrary", "arbitrary")),
        name="outproj",
    )(o, gate, gain, w_out, x, g1, sc2, sh2, w_route, b_route)


def _route_tile(lg, tri_strict, counts):
    lane_i = lax.broadcasted_iota(jnp.int32, lg.shape, 1)
    lane = lane_i.astype(F32)
    big = 1e9

    def first_argmax(v, vmax):
        return jnp.min(jnp.where(v == vmax, lane, big), axis=1, keepdims=True)

    is_group = lane < N_GROUPS
    gl = jnp.where(is_group, lg, NEG_BIG)
    g_max = jnp.max(gl, axis=1, keepdims=True)
    g_sum = jnp.sum(jnp.where(is_group, jnp.exp(gl - g_max), 0.0), axis=1, keepdims=True)
    g_p = 1.0 / g_sum
    g_idx = first_argmax(gl, g_max)
    lo = N_GROUPS + EXPERTS_PER_GROUP * g_idx
    el = jnp.where((lane >= lo) & (lane < lo + EXPERTS_PER_GROUP), lg, NEG_BIG)
    m1 = jnp.max(el, axis=1, keepdims=True)
    i1 = first_argmax(el, m1)
    el2 = jnp.where(lane == i1, NEG_BIG, el)
    m2 = jnp.max(el2, axis=1, keepdims=True)
    i2 = first_argmax(el2, m2)
    e = jnp.exp(m2 - m1)
    w1 = g_p / (1.0 + e)
    w2 = w1 * e
    e1 = i1 - N_GROUPS
    e2 = i2 - N_GROUPS
    hot1 = (lane == e1).astype(F32)
    hot2 = (lane == e2).astype(F32)
    hot = hot1 + hot2
    before = _dot(tri_strict, hot.astype(BF16)) + counts
    rank1 = jnp.sum(before * hot1, axis=1, keepdims=True)
    rank2 = jnp.sum(before * hot2, axis=1, keepdims=True)
    meta = jnp.where(lane_i == 0, e1, jnp.where(lane_i == 1, e2, jnp.where(lane_i == 2, rank1, rank2)))
    return (meta.astype(jnp.int32), jnp.where(lane_i == 0, w1, w2),
            counts + jnp.sum(hot, axis=0, keepdims=True))


def _index_windows(idx):
    n = idx.shape[0]
    return jnp.zeros((n // SC_WINDOW, LANES), jnp.int32).at[:, :SC_WINDOW].set(
        idx.reshape(n // SC_WINDOW, SC_WINDOW))


def _sc_mesh():
    return plsc.VectorSubcoreMesh(core_axis_name="core", subcore_axis_name="subcore")


def _scatter_rows(x, idx, n_rows):
    n_in, d = x.shape
    n_idx = idx.shape[0]

    def program(x_hbm, i_hbm, o_hbm):
        def window(x_vmem, i_vmem):
            pltpu.sync_copy(x_vmem, o_hbm.at[i_vmem.at[0, pl.ds(0, SC_WINDOW)]])

        pltpu.emit_pipeline(
            window,
            grid=(n_idx // SC_WINDOW,),
            in_specs=[pl.BlockSpec((SC_WINDOW, d), lambda i: (i % (n_in // SC_WINDOW), 0)),
                      pl.BlockSpec((1, LANES), lambda i: (i, 0))],
            out_specs=[],
            core_axis_name=("core", "subcore"),
            dimension_semantics=(pltpu.PARALLEL,),
        )(x_hbm, i_hbm)

    return pl.kernel(program, out_type=jax.ShapeDtypeStruct((n_rows, d), x.dtype), mesh=_sc_mesh(),
                     name="moe_scatter_rows")(x, _index_windows(idx))


def _gather_rows(x, idx):
    d = x.shape[1]
    n_idx = idx.shape[0]

    def program(x_hbm, i_hbm, o_hbm):
        def window(i_vmem, o_vmem):
            pltpu.sync_copy(x_hbm.at[i_vmem.at[0, pl.ds(0, SC_WINDOW)]], o_vmem)

        pltpu.emit_pipeline(
            window,
            grid=(n_idx // SC_WINDOW,),
            in_specs=[pl.BlockSpec((1, LANES), lambda i: (i, 0))],
            out_specs=[pl.BlockSpec((SC_WINDOW, d), lambda i: (i, 0))],
            core_axis_name=("core", "subcore"),
            dimension_semantics=(pltpu.PARALLEL,),
        )(i_hbm, o_hbm)

    return pl.kernel(program, out_type=jax.ShapeDtypeStruct((n_idx, d), x.dtype), mesh=_sc_mesh(),
                     name="moe_gather_rows")(x, _index_windows(idx))


def _gmm_kernel(first_ref, count_ref, xs_ref, wgu_ref, wd_ref, ys_ref, wgu_bf, wd_bf, xbuf, obuf, sem_in, sem_out):
    e = pl.program_id(0)
    last = pl.num_programs(0) - 1
    slots = xbuf.shape[0]
    tm = xbuf.shape[1]
    half = xbuf.shape[2]
    f = wd_bf.shape[0]
    n_tiles = count_ref[e]
    first = first_ref[e]
    total = first_ref[last] + count_ref[last]

    def rows(g):
        return pl.ds(pl.multiple_of(g * tm, tm), tm)

    def load(g):
        return pltpu.make_async_copy(xs_ref.at[rows(g)], xbuf.at[g % slots], sem_in.at[g % slots])

    def store(g):
        return pltpu.make_async_copy(obuf.at[g % slots], ys_ref.at[rows(g)], sem_out.at[g % slots])

    @pl.when(e == 0)
    def _():
        for g in range(slots - 1):
            @pl.when(g < total)
            def _():
                load(g).start()

    @pl.when(n_tiles > 0)
    def _():
        wgu_bf[...] = wgu_ref[0].astype(BF16)
        wd_bf[...] = wd_ref[0].astype(BF16)

    def tile(t, carry):
        g = first + t
        load(g).wait()

        @pl.when(g + slots - 1 < total)
        def _():
            load(g + slots - 1).start()

        @pl.when(g >= slots)
        def _():
            store(g - slots).wait()

        x_hi, x_lo = _unpack_bf16_pairs(xbuf[g % slots])
        hu = _dot(x_hi.astype(BF16), wgu_bf[:half, :]) + _dot(x_lo.astype(BF16), wgu_bf[half:, :])
        act = _silu(hu[:, :f]) * hu[:, f:]
        obuf[g % slots] = _pack_bf16_pairs(_dot(act.astype(BF16), wd_bf[...]))
        store(g).start()
        return carry

    lax.fori_loop(0, n_tiles, tile, 0)

    @pl.when(e == last)
    def _():
        for j in range(slots):
            g = total - slots + j

            @pl.when(g >= 0)
            def _():
                store(g).wait()


def _gmm(layer, tile_first, tile_count, xs, w_gate_up, w_down):
    n_rows, half = xs.shape
    d = 2 * half
    tm = MOE_TILE
    f2 = w_gate_up.shape[2]
    grid_spec = pltpu.PrefetchScalarGridSpec(
        num_scalar_prefetch=2,
        grid=(N_EXPERTS,),
        in_specs=[
            pl.BlockSpec(memory_space=pl.ANY),
            pl.BlockSpec((1, d, f2), lambda e, tf, tc: (layer * N_EXPERTS + e, 0, 0)),
            pl.BlockSpec((1, f2 // 2, d), lambda e, tf, tc: (layer * N_EXPERTS + e, 0, 0)),
        ],
        out_specs=pl.BlockSpec(memory_space=pl.ANY),
        scratch_shapes=[
            pltpu.VMEM((d, f2), BF16), pltpu.VMEM((f2 // 2, d), BF16),
            pltpu.VMEM((GMM_SLOTS, tm, half), PACKED), pltpu.VMEM((GMM_SLOTS, tm, half), PACKED),
            pltpu.SemaphoreType.DMA((GMM_SLOTS,)), pltpu.SemaphoreType.DMA((GMM_SLOTS,)),
        ],
    )
    return pl.pallas_call(
        _gmm_kernel,
        grid_spec=grid_spec,
        out_shape=jax.ShapeDtypeStruct((n_rows, half), PACKED),
        input_output_aliases={2: 0},
        compiler_params=_params(("arbitrary",)),
        name="moe_gmm",
    )(tile_first, tile_count, xs, w_gate_up, w_down)


def _combine_kernel(y0_ref, y1_ref, wt_ref, x_ref, g2_ref, fg_ref, o_ref):
    x2 = x_ref[...] + g2_ref[0] * _combine_rows(y0_ref[...], y1_ref[...], wt_ref[...])
    o_ref[...] = x2 * lax.rsqrt(jnp.mean(x2 * x2, axis=-1, keepdims=True) + EPS) * fg_ref[...]


def _combine_final(yg, wt, x1, g2, final_gain, tokens_per_seq):
    t, d = x1.shape
    tt = min(TOKEN_TILE, tokens_per_seq)
    nb = t // tt
    per_seq = tokens_per_seq // tt
    return pl.pallas_call(
        _combine_kernel,
        grid=(nb,),
        in_specs=[
            pl.BlockSpec((tt, d // 2), lambda i: (i, 0)),
            pl.BlockSpec((tt, d // 2), lambda i: (i + nb, 0)),
            pl.BlockSpec((tt, LANES), lambda i: (i, 0)),
            pl.BlockSpec((tt, d), lambda i: (i, 0)),
            pl.BlockSpec((1, 1, d), lambda i: (i // per_seq, 0, 0)),
            pl.BlockSpec((1, d), lambda i: (0, 0)),
        ],
        out_specs=pl.BlockSpec((tt, d), lambda i: (i, 0)),
        out_shape=jax.ShapeDtypeStruct((t, d), F32),
        compiler_params=_params(("arbitrary",)),
        name="moe_combine",
    )(yg, yg, wt, x1, g2, final_gain)


def _sorted_positions(meta, counts, tile):
    counts = counts[0, :N_EXPERTS].astype(jnp.int32)
    tiles_per = (counts + tile - 1) // tile
    tile_first = jnp.cumsum(tiles_per) - tiles_per
    choice_major = lambda a: jnp.transpose(a, (1, 0, 2)).reshape(TOP_K, -1)
    eid = choice_major(meta[:, :TOP_K, :])
    rank = choice_major(meta[:, TOP_K:2 * TOP_K, :])
    onehot = (eid[..., None] == jnp.arange(N_EXPERTS, dtype=jnp.int32)).astype(jnp.int32)
    pos = jnp.sum(onehot * (tile_first * tile), axis=-1) + rank
    return pos.astype(jnp.int32), tile_first.astype(jnp.int32), tiles_per.astype(jnp.int32)


def _sorted_rows(n_tokens, tile):
    return (TOP_K * n_tokens // tile + N_EXPERTS) * tile


def _hier_moe(layer, h2, meta, counts, w_gate_up, w_down):
    pos, tile_first, tile_count = _sorted_positions(meta, counts, MOE_TILE)
    pos = pos.reshape(-1)
    xs = _scatter_rows(h2, pos, _sorted_rows(h2.shape[0], MOE_TILE))
    ys = _gmm(layer, tile_first, tile_count, xs, w_gate_up, w_down)
    return _gather_rows(ys, pos)


def _hi_lo_cols(w, n):
    k, m = w.shape
    wp = jnp.zeros((k, n), F32).at[:, :m].set(w)
    hi = wp.astype(BF16)
    lo = (wp - hi.astype(F32)).astype(BF16)
    return jnp.concatenate([hi, lo], axis=1)


def _pad_row(v, n):
    return jnp.zeros((1, n), F32).at[0, :v.shape[0]].set(v)


def kernel(x, c, ada_w, ada_b, gdn_w_in, gdn_conv, gdn_a_log, gdn_dt_bias, gdn_norm, gdn_w_out,
           fox_w_in, fox_b_f, fox_q_norm, fox_k_norm, fox_w_out,
           moe_w_group, moe_b_group, moe_w_router, moe_b_router, moe_w_gate_up, moe_w_down,
           final_norm):
    b, l, d = x.shape
    depth = ada_w.shape[0]
    t = b * l
    ada = _ada(c, ada_w, ada_b)
    w_gate_up = moe_w_gate_up.reshape(depth * N_EXPERTS, d, moe_w_gate_up.shape[-1])
    w_down = moe_w_down.reshape(depth * N_EXPERTS, moe_w_down.shape[-2], d)
    final_gain = final_norm.reshape(1, d)

    x_src = x
    for i in range(depth):
        sh1, sc1, g1, sh2, sc2, g2 = [ada[i, :, s * d:(s + 1) * d].reshape(b, 1, d) for s in range(6)]
        j = i // 2
        w_route = _hi_lo_cols(jnp.concatenate([moe_w_group[i], moe_w_router[i]], axis=1), LANES)
        b_route = _pad_row(jnp.concatenate([moe_b_group[i], moe_b_router[i]]), LANES)
        if i % 2 == 0:
            w_in = gdn_w_in[j]
            q, k, v, gate, gb, *x_new = _gdn_inproj(
                x_src, sc1, sh1, w_in.astype(BF16), _hi_lo_cols(w_in[:, 4 * d:], LANES),
                gdn_conv[j], jnp.concatenate([_pad_row(gdn_a_log[j], LANES), _pad_row(gdn_dt_bias[j], LANES)]))
            x = x_new[0] if x_new else x_src
            o = _gdn_chunk(q, k, v, gb)
            gain = jnp.tile(gdn_norm[j], N_HEADS_GDN).reshape(1, d)
            x1, h2, meta, wt, counts = _outproj(True, o, gate, gain, gdn_w_out[j].astype(BF16), x, g1, sc2, sh2,
                                      w_route, b_route)
        else:
            w_in = fox_w_in[j]
            q, k, v, gate, cum_t, *x_new = _fox_inproj(
                x_src, sc1, sh1, w_in.astype(BF16), _hi_lo_cols(w_in[:, 4 * d:], LANES),
                _pad_row(fox_b_f[j], LANES))
            x = x_new[0] if x_new else x_src
            cum =cum_t[:, :N_HEADS_FOX, :].reshape(b, N_HEADS_FOX // 2, 2, l)
            q_gain = jnp.tile(fox_q_norm[j], 2).reshape(1, LANES)
            k_gain = jnp.tile(fox_k_norm[j], 2).reshape(1, LANES)
            o = _fox_attn(q, k, v, cum, q_gain, k_gain)
            x1, h2, meta, wt, counts = _outproj(False, o, gate, final_gain, fox_w_out[j].astype(BF16), x, g1, sc2, sh2,
                                      w_route, b_route)
        yg = _hier_moe(i, h2.reshape(t, d // 2), meta, counts, w_gate_up, w_down)
        x_src = (x1, yg, wt.reshape(t, LANES), g2)
    x1, yg, wt, g2 = x_src
    return _combine_final(yg, wt, x1.reshape(t, d), g2, final_gain, l).reshape(b, l, d)
```

```python
import functools

import jax
import jax.numpy as jnp
from jax import lax
from jax.experimental import pallas as pl
from jax.experimental.pallas import tpu as pltpu
from jax.experimental.pallas import tpu_sc as plsc

F32 = jnp.float32
BF16 = jnp.bfloat16
PACKED = jnp.int32

EPS = 1e-6
CHUNK = 64
N_HEADS_GDN = 8
HEAD_DIM_GDN = 128
N_HEADS_FOX = 16
HEAD_DIM_FOX = 64
CONV_K = 4
N_GROUPS = 4
EXPERTS_PER_GROUP = 8
N_EXPERTS = N_GROUPS * EXPERTS_PER_GROUP
TOP_K = 2

LANES = 128
SUBLANES = 8
VMEM_LIMIT_BYTES = 56 * 1024 * 1024

ROW_TILE = 512
ATTN_TILE = 512
ATTN_UNROLL = 2
GDN_BLOCK = 512
GDN_CHUNKS_PER_ITER = 4
GDN_HEADS_PACKED = 4
MOE_TILE = 256
GMM_SLOTS = 4
TOKEN_TILE = 512
SC_WINDOW = 32
NEG_BIG = -1e30
LOG2E = 1.4426950408889634


def _params(semantics):
    return pltpu.CompilerParams(dimension_semantics=semantics, vmem_limit_bytes=VMEM_LIMIT_BYTES)


def _sigmoid(x):
    return 0.5 * jnp.tanh(0.5 * x) + 0.5


def _silu(x):
    return x * _sigmoid(x)


def _softplus(x):
    return jnp.maximum(x, 0.0) + jnp.log(1.0 + jnp.exp(-jnp.abs(x)))


def _split2(a):
    hi = a.astype(BF16)
    lo = (a - hi.astype(F32)).astype(BF16)
    return hi, lo


def _split3(a):
    hi = a.astype(BF16)
    r = a - hi.astype(F32)
    mid = r.astype(BF16)
    lo = (r - mid.astype(F32)).astype(BF16)
    return hi, mid, lo


def _dot(a, b):
    return jnp.dot(a, b, preferred_element_type=F32)


def _dot_nt(a, b):
    return lax.dot_general(a, b, (((1,), (1,)), ((), ())), preferred_element_type=F32)


def _dot_split_weights(a, w_hi_lo, n):
    a_hi, a_lo = _split2(a)
    r = _dot(a_hi, w_hi_lo)
    return r[:, :n] + r[:, n:] + _dot(a_lo, w_hi_lo[:, :n])


def _cumsum_rows(tri_bf16, v):
    hi, mid, lo = _split3(v)
    r = _dot(tri_bf16, jnp.concatenate([hi, mid], axis=1))
    return r[:, :LANES] + r[:, LANES:] + _dot(tri_bf16, lo)


def _rms_mod(x, sc, sh):
    ms = jnp.mean(x * x, axis=-1, keepdims=True)
    return x * lax.rsqrt(ms + EPS) * (1.0 + sc) + sh


def _pack_bf16_pairs(a):
    n = a.shape[1] // 2
    bits = lax.bitcast_convert_type(a.astype(BF16).astype(F32), jnp.uint32)
    words = (bits[:, :n] & jnp.uint32(0xFFFF0000)) | (bits[:, n:] >> 16)
    return lax.bitcast_convert_type(words, PACKED)


def _unpack_bf16_pairs(p):
    words = lax.bitcast_convert_type(p, jnp.uint32)
    hi = lax.bitcast_convert_type(words & jnp.uint32(0xFFFF0000), F32)
    lo = lax.bitcast_convert_type(words << 16, F32)
    return hi, lo


def _lane_col(v, idx, lane):
    return jnp.sum(jnp.where(lane == idx, v, 0.0), axis=1, keepdims=True)


def _ada_kernel(c_ref, w_ref, b_ref, o_ref):
    c = c_ref[...]
    o_ref[0] = _dot(_silu(c).astype(BF16), w_ref[0].astype(BF16)) + b_ref[0]


def _ada(c, ada_w, ada_b):
    depth, d, n = ada_w.shape
    b = c.shape[0]
    tn = 1536
    return pl.pallas_call(
        _ada_kernel,
        grid=(depth, n // tn),
        in_specs=[
            pl.BlockSpec((b, d), lambda i, j: (0, 0)),
            pl.BlockSpec((1, d, tn), lambda i, j: (i, 0, j)),
            pl.BlockSpec((1, 1, tn), lambda i, j: (i, 0, j)),
        ],
        out_specs=pl.BlockSpec((1, b, tn), lambda i, j: (i, 0, j)),
        out_shape=jax.ShapeDtypeStruct((depth, b, n), F32),
        compiler_params=_params(("arbitrary", "arbitrary")),
        name="ada",
    )(c, ada_w, ada_b.reshape(depth, 1, n))


def _combine_rows(y0, y1, wt):
    lane = lax.broadcasted_iota(jnp.int32, wt.shape, 1)
    w0 = _lane_col(wt, 0, lane)
    w1 = _lane_col(wt, 1, lane)
    a_hi, a_lo = _unpack_bf16_pairs(y0)
    b_hi, b_lo = _unpack_bf16_pairs(y1)
    return jnp.concatenate([w0 * a_hi + w1 * b_hi, w0 * a_lo + w1 * b_lo], axis=1)


N_PENDING_REFS = 5


def _residual_tile(pending, refs):
    if not pending:
        return refs[0][0]
    x1_ref, y0_ref, y1_ref, wt_ref, g2_ref = refs
    return x1_ref[0] + g2_ref[0] * _combine_rows(y0_ref[...], y1_ref[...], wt_ref[...])


def _gdn_inproj_kernel(pending, *refs):
    nx = N_PENDING_REFS if pending else 1
    sc_ref, sh_ref, w_ref, wab_ref, conv_ref, prm_ref, q_ref, k_ref, v_ref, gate_ref, gb_ref = refs[nx:nx + 11]
    halo_ref = refs[-1]
    x = _residual_tile(pending, refs[:nx])
    if pending:
        refs[nx + 11][0] = x
    tm, d = x.shape
    h = _rms_mod(x, sc_ref[0], sh_ref[0])
    hb = h.astype(BF16)

    @pl.when(pl.program_id(1) == 0)
    def _():
        halo_ref[:, 0:SUBLANES, :] = jnp.zeros((3, SUBLANES, d), F32)

    for s, o_ref in enumerate((q_ref, k_ref, v_ref)):
        raw = _dot(hb, w_ref[:, s * d:(s + 1) * d])
        halo_ref[s, SUBLANES:SUBLANES + tm, :] = raw
        cw = conv_ref[:, s * d:(s + 1) * d]
        y = raw * cw[3:4]
        for j in range(CONV_K - 1):
            off = SUBLANES - (CONV_K - 1) + j
            y = y + halo_ref[s, off:off + tm, :] * cw[j:j + 1]
        halo_ref[s, 0:SUBLANES, :] = halo_ref[s, tm:tm + SUBLANES, :]
        y = _silu(y)
        if s < 2:
            scale = HEAD_DIM_GDN ** -0.5 if s == 0 else 1.0
            for hh in range(N_HEADS_GDN):
                seg = y[:, hh * HEAD_DIM_GDN:(hh + 1) * HEAD_DIM_GDN]
                inv = lax.rsqrt(jnp.sum(seg * seg, axis=-1, keepdims=True) + EPS) * scale
                o_ref[0, :, hh * HEAD_DIM_GDN:(hh + 1) * HEAD_DIM_GDN] = (seg * inv).astype(o_ref.dtype)
        else:
            o_ref[0] = y.astype(o_ref.dtype)

    gate_ref[0] = _silu(_dot(hb, w_ref[:, 3 * d:4 * d])).astype(gate_ref.dtype)

    ab = _dot_split_weights(h, wab_ref[...], LANES)
    log_alpha = -jnp.exp(prm_ref[0:1, :]) * _softplus(ab + prm_ref[1:2, :])
    lane = lax.broadcasted_iota(jnp.int32, ab.shape, 1)
    gb_ref[0] = jnp.where(lane < N_HEADS_GDN, log_alpha, _sigmoid(ab))


def _residual_operands(x_src, tm):
    row = lambda i, j: (i, j, 0)
    if not isinstance(x_src, tuple):
        b, l, d = x_src.shape
        return False, [x_src], [pl.BlockSpec((1, tm, d), row)], [], []
    x1, yg, wt, g2 = x_src
    b, l, d = x1.shape
    per_seq = l // tm
    n_tiles = b * per_seq
    specs = [
        pl.BlockSpec((1, tm, d), row),
        pl.BlockSpec((tm, d // 2), lambda i, j: (i * per_seq + j, 0)),
        pl.BlockSpec((tm, d // 2), lambda i, j: (i * per_seq + j + n_tiles, 0)),
        pl.BlockSpec((tm, LANES), lambda i, j: (i * per_seq + j, 0)),
        pl.BlockSpec((1, 1, d), lambda i, j: (i, 0, 0)),
    ]
    return (True, [x1, yg, yg, wt, g2], specs, [jax.ShapeDtypeStruct((b, l, d), F32)],
            [pl.BlockSpec((1, tm, d), row)])


def _gdn_inproj(x_src, sc, sh, w_main, w_ab, conv_w, prm):
    b, l, d = (x_src[0] if isinstance(x_src, tuple) else x_src).shape
    tm = min(ROW_TILE, l)
    pending, x_ops, x_specs, x_out_shape, x_out_spec = _residual_operands(x_src, tm)
    act = jax.ShapeDtypeStruct((b, l, d), BF16)
    row = lambda i, j: (i, j, 0)
    vec = lambda i, j: (i, 0, 0)
    const = lambda i, j: (0, 0)
    return pl.pallas_call(
        functools.partial(_gdn_inproj_kernel, pending),
        grid=(b, l // tm),
        in_specs=x_specs + [
            pl.BlockSpec((1, 1, d), vec),
            pl.BlockSpec((1, 1, d), vec),
            pl.BlockSpec(w_main.shape, const),
            pl.BlockSpec(w_ab.shape, const),
            pl.BlockSpec(conv_w.shape, const),
            pl.BlockSpec(prm.shape, const),
        ],
        out_specs=[pl.BlockSpec((1, tm, d), row)] * 4 + [pl.BlockSpec((1, tm, LANES), row)] + x_out_spec,
        out_shape=[act, act, act, act, jax.ShapeDtypeStruct((b, l, LANES), F32)] + x_out_shape,
        scratch_shapes=[pltpu.VMEM((3, tm + SUBLANES, d), F32)],
        compiler_params=_params(("arbitrary", "arbitrary")),
        name="gdn_inproj",
    )(*x_ops, sc, sh, w_main, w_ab, conv_w, prm)


def _block_diag(r, blk, n_blk):
    return jnp.concatenate([jnp.where(blk == b, r, jnp.zeros_like(r)) for b in range(n_blk)], axis=0)


def _mm_packed(lhs, rhs, blk, n_blk):
    m = lhs.shape[0]
    l_hi, l_lo = _split2(lhs)
    r_hi, r_lo = _split2(rhs)
    t = _dot(jnp.concatenate([l_hi, l_lo], axis=0), _block_diag(r_hi, blk, n_blk))
    return t[:m] + t[m:] + _dot(l_hi, _block_diag(r_lo, blk, n_blk))


def _unit_lower_inverse(a_strict, eye, blk, n_blk):
    n = a_strict[0].shape[0]
    levels = n.bit_length() - 1
    m = [-a for a in a_strict]
    x = [eye + mi for mi in m]
    p = [_mm_packed(mi, mi, blk, n_blk) for mi in m]
    for _ in range(levels - 2):
        r = [_mm_packed(jnp.concatenate([pi, xi], axis=0), pi, blk, n_blk) for pi, xi in zip(p, x)]
        x = [xi + ri[n:] for xi, ri in zip(x, r)]
        p = [ri[:n] for ri in r]
    return [xi + _mm_packed(xi, pi, blk, n_blk) for xi, pi in zip(x, p)]


def _gdn_chunk_kernel(q_ref, k_ref, v_ref, gb_ref, o_ref, s_ref):
    lb = q_ref.shape[1]
    c = CHUNK
    dk = HEAD_DIM_GDN
    nc = GDN_CHUNKS_PER_ITER
    heads = range(N_HEADS_GDN)
    sl = [slice(hh * dk, (hh + 1) * dk) for hh in heads]

    @pl.when(pl.program_id(1) == 0)
    def _():
        s_ref[...] = jnp.zeros(s_ref.shape, F32)

    pk = GDN_HEADS_PACKED
    groups = N_HEADS_GDN // pk
    tri = (lax.broadcasted_iota(jnp.int32, (c, c), 0) >= lax.broadcasted_iota(jnp.int32, (c, c), 1)).astype(BF16)
    lane = lax.broadcasted_iota(jnp.int32, (c, LANES), 1)
    side = lane // c
    row_p = lax.broadcasted_iota(jnp.int32, (c, pk * c), 0)
    lane_p = lax.broadcasted_iota(jnp.int32, (c, pk * c), 1)
    blk = lane_p // c
    col_p = lane_p - blk * c
    incl = row_p >= col_p
    strict = row_p > col_p
    diag = row_p == col_p
    eye = diag.astype(F32)
    blk_k = lax.broadcasted_iota(jnp.int32, (c, pk * dk), 1) // dk

    def per_head(cols, width):
        return jnp.concatenate([jnp.broadcast_to(col, (c, width)) for col in cols], axis=1)

    def body(it, carry):
        r0, q, kd_t, egl_last, rhs_pair, kq, decay, eg_cols = [], [], [], [], [], [], [], []
        for j in range(nc):
            r = pl.multiple_of((it * nc + j) * c, c)
            r0.append(r)
            gb = gb_ref[0, pl.ds(r, c), :]
            g = _cumsum_rows(tri, gb)
            g_last = g[c - 1:c, :]
            eg = jnp.exp(g)
            egl = jnp.exp(g_last - g)
            eg_last = jnp.exp(g_last)
            for gi in range(groups):
                hs = [gi * pk + b for b in range(pk)]
                cols = slice(gi * pk * dk, (gi + 1) * pk * dk)
                q4 = q_ref[0, pl.ds(r, c), cols].astype(F32)
                k4 = k_ref[0, pl.ds(r, c), cols].astype(F32)
                v4 = v_ref[0, pl.ds(r, c), cols].astype(F32)
                beta4 = per_head([_lane_col(gb, N_HEADS_GDN + hh, lane) for hh in hs], dk)
                eg4 = per_head([_lane_col(eg, hh, lane) for hh in hs], dk)
                egl4 = per_head([_lane_col(egl, hh, lane) for hh in hs], dk)
                kb4 = k4 * beta4
                k4b = k4.astype(BF16)
                k_diag = jnp.concatenate([jnp.where(blk_k == b, k4b, jnp.zeros_like(k4b)) for b in range(pk)],
                                         axis=0)
                kq.append(_dot_nt(jnp.concatenate([kb4, q4], axis=0).astype(BF16), k_diag))
                g_col = per_head([_lane_col(g, hh, lane) for hh in hs], c)
                g_row = jnp.sum(jnp.where(diag, g_col, 0.0), axis=0, keepdims=True)
                decay.append(jnp.where(incl, jnp.exp(jnp.where(incl, g_col - g_row, 0.0)), 0.0))
                vb4 = v4 * beta4
                kbe4 = kb4 * eg4
                qe4 = q4 * eg4
                kd4 = k4 * egl4
                for b, hh in enumerate(hs):
                    hsl = slice(b * dk, (b + 1) * dk)
                    q.append(qe4[:, hsl])
                    kd_t.append(kd4[:, hsl].T.astype(BF16))
                    egl_last.append(_lane_col(eg_last, hh, lane[0:1]))
                for p in range(pk // 2):
                    a = slice(2 * p * dk, (2 * p + 1) * dk)
                    bsl = slice((2 * p + 1) * dk, (2 * p + 2) * dk)
                    rhs_pair.append(jnp.concatenate(
                        [jnp.concatenate([vb4[:, a], kbe4[:, a]], axis=1),
                         jnp.concatenate([vb4[:, bsl], kbe4[:, bsl]], axis=1)], axis=0).astype(BF16))
        n_grp = len(kq)
        t_inv = _unit_lower_inverse([jnp.where(strict, kq[i][:c] * decay[i], 0.0) for i in range(n_grp)],
                                    eye, blk, pk)
        qk_p = [kq[i][c:] * decay[i] for i in range(n_grp)]

        def head_lhs(packed, i_grp, b):
            tile = packed[i_grp][:, (b // 2) * LANES:(b // 2 + 1) * LANES]
            return jnp.where(side == b % 2, tile, 0.0).astype(BF16)

        uw, qk = [], []
        for i_grp in range(n_grp):
            for b in range(pk):
                uw.append(_dot(head_lhs(t_inv, i_grp, b), rhs_pair[i_grp * (pk // 2) + b // 2]))
                qk.append(head_lhs(qk_p, i_grp, b))
        n = len(uw)
        wq = [jnp.concatenate([uw[i][:, dk:], q[i]], axis=0).astype(BF16) for i in range(n)]

        s = [s_ref[hh] for hh in heads]
        for j in range(nc):
            idx = [j * N_HEADS_GDN + hh for hh in heads]
            ws_qs = [_dot(wq[i], s[hh].astype(BF16)) for hh, i in zip(heads, idx)]
            v_new = [(uw[i][:, :dk] - ws_qs[hh][:c]).astype(BF16) for hh, i in zip(heads, idx)]
            for hh, i in zip(heads, idx):
                pair = jnp.concatenate([v_new[hh - hh % 2], v_new[hh - hh % 2 + 1]], axis=0)
                o = ws_qs[hh][c:] + _dot(qk[i], pair)
                o_ref[0, pl.ds(r0[j], c), sl[hh]] = o.astype(o_ref.dtype)
            s = [s[hh] * egl_last[i] + _dot(kd_t[i], v_new[hh]) for hh, i in zip(heads, idx)]
        for hh in heads:
            s_ref[hh] = s[hh]
        return carry

    lax.fori_loop(0, lb // (c * nc), body, 0)


def _gdn_chunk(q, k, v, gb):
    b, l, d = q.shape
    lb = min(GDN_BLOCK, l)
    row = lambda i, j: (i, j, 0)
    return pl.pallas_call(
        _gdn_chunk_kernel,
        grid=(b, l // lb),
        in_specs=[pl.BlockSpec((1, lb, d), row)] * 3 + [pl.BlockSpec((1, lb, LANES), row)],
        out_specs=pl.BlockSpec((1, lb, d), row),
        out_shape=jax.ShapeDtypeStruct((b, l, d), F32),
        scratch_shapes=[pltpu.VMEM((N_HEADS_GDN, HEAD_DIM_GDN, HEAD_DIM_GDN), F32)],
        compiler_params=_params(("arbitrary", "arbitrary")),
        name="gdn_chunk",
    )(q, k, v, gb)


def _fox_inproj_kernel(pending, *refs):
    nx = N_PENDING_REFS if pending else 1
    sc_ref, sh_ref, w_ref, wf_ref, bf_ref, q_ref, k_ref, v_ref, g_ref, cum_t_ref = refs[nx:nx + 10]
    carry_ref = refs[-1]
    x = _residual_tile(pending, refs[:nx])
    if pending:
        refs[nx + 10][0] = x
    tm, d = x.shape
    h = _rms_mod(x, sc_ref[0], sh_ref[0])
    hb = h.astype(BF16)
    for s, o_ref in enumerate((q_ref, k_ref, v_ref)):
        o_ref[0] = _dot(hb, w_ref[:, s * d:(s + 1) * d]).astype(o_ref.dtype)
    g_ref[0] = _sigmoid(_dot(hb, w_ref[:, 3 * d:4 * d])).astype(g_ref.dtype)

    @pl.when(pl.program_id(1) == 0)
    def _():
        carry_ref[...] = jnp.zeros(carry_ref.shape, F32)

    f_logit = _dot_split_weights(h, wf_ref[...], LANES) + bf_ref[...]
    log_f = -_softplus(-f_logit)
    row = lax.broadcasted_iota(jnp.int32, (tm, tm), 0)
    col = lax.broadcasted_iota(jnp.int32, (tm, tm), 1)
    cum = _cumsum_rows((row >= col).astype(BF16), log_f) + carry_ref[...]
    carry_ref[...] = cum[tm - 1:tm, :]
    cum_t_ref[0] = cum.T


def _fox_inproj(x_src, sc, sh, w_main, w_f, b_f):
    b, l, d = (x_src[0] if isinstance(x_src, tuple) else x_src).shape
    tm = min(ROW_TILE, l)
    pending, x_ops, x_specs, x_out_shape, x_out_spec = _residual_operands(x_src, tm)
    act = jax.ShapeDtypeStruct((b, l, d), BF16)
    row = lambda i, j: (i, j, 0)
    vec = lambda i, j: (i, 0, 0)
    const = lambda i, j: (0, 0)
    return pl.pallas_call(
        functools.partial(_fox_inproj_kernel, pending),
        grid=(b, l // tm),
        in_specs=x_specs + [
            pl.BlockSpec((1, 1, d), vec),
            pl.BlockSpec((1, 1, d), vec),
            pl.BlockSpec(w_main.shape, const),
            pl.BlockSpec(w_f.shape, const),
            pl.BlockSpec(b_f.shape, const),
        ],
        out_specs=([pl.BlockSpec((1, tm, d), row)] * 4 + [pl.BlockSpec((1, LANES, tm), lambda i, j: (i, 0, j))]
                   + x_out_spec),
        out_shape=[act, act, act, act, jax.ShapeDtypeStruct((b, LANES, l), F32)] + x_out_shape,
        scratch_shapes=[pltpu.VMEM((1, LANES), F32)],
        compiler_params=_params(("arbitrary", "arbitrary")),
        name="fox_inproj",
    )(*x_ops, sc, sh, w_main, w_f, b_f)


def _fox_attn_kernel(q_ref, k_ref, v_ref, cum_ref, qg_ref, kg_ref, o_ref, kn_ref, s_ref, mx_ref, ls_ref, acc_ref):
    tq = s_ref.shape[2]
    hd = HEAD_DIM_FOX
    lane = lax.broadcasted_iota(jnp.int32, (1, LANES), 1)
    first = lane < hd
    n_col = tq // LANES

    def head_norm(t, gain):
        sq = t * t
        s0 = jnp.sum(jnp.where(first, sq, 0.0), axis=1, keepdims=True)
        s1 = jnp.sum(jnp.where(first, 0.0, sq), axis=1, keepdims=True)
        ms = jnp.where(first, s0, s1) * (1.0 / hd)
        return t * lax.rsqrt(ms + EPS) * gain

    kn_ref[...] = head_norm(k_ref[0].astype(F32), kg_ref[...]).astype(kn_ref.dtype)

    def q_tile(qi, carry):
        q0 = qi * tq
        qn = head_norm(q_ref[0, pl.ds(q0, tq), :].astype(F32), qg_ref[...]) * (hd ** -0.5 * LOG2E)
        q2 = jnp.concatenate([jnp.where(first, qn, 0.0), jnp.where(first, 0.0, qn)], axis=0).astype(BF16)

        def scores(j, masked):
            k0 = j * tq if isinstance(j, int) else pl.multiple_of(j * tq, tq)
            ck = cum_ref[0, 0, :, pl.ds(k0, tq)] * LOG2E
            s = _dot_nt(q2, kn_ref[pl.ds(k0, tq), :])
            s = jnp.concatenate([s[:tq] - ck[0:1], s[tq:] - ck[1:2]], axis=0)
            if masked:
                r = lax.broadcasted_iota(jnp.int32, (tq, tq), 0)
                c = lax.broadcasted_iota(jnp.int32, (tq, tq), 1)
                keep = jnp.concatenate([r >= c, r >= c], axis=0)
                s = jnp.where(keep, s, NEG_BIG)
            s_ref[j] = s
            mx = mx_ref[...]
            for t in range(n_col):
                mx = jnp.maximum(mx, s[:, t * LANES:(t + 1) * LANES])
            mx_ref[...] = mx

        def scores_step(j, c):
            scores(j, False)
            return c

        mx_ref[...] = jnp.full(mx_ref.shape, NEG_BIG, F32)
        if qi > 0:
            lax.fori_loop(0, qi, scores_step, 0, unroll=min(qi, ATTN_UNROLL))
        scores(qi, True)
        mx_ref[...] = jnp.broadcast_to(jnp.max(mx_ref[...], axis=1, keepdims=True), mx_ref.shape)

        acc_ref[...] = jnp.zeros(acc_ref.shape, F32)
        ls_ref[...] = jnp.zeros(ls_ref.shape, F32)

        def weighted_sum(j, c):
            k0 = j * tq if isinstance(j, int) else pl.multiple_of(j * tq, tq)
            m = mx_ref[...]
            s = s_ref[j]
            p = jnp.concatenate([jnp.exp2(s[:, t * LANES:(t + 1) * LANES] - m) for t in range(n_col)], axis=1)
            acc_ref[...] += _dot(p.astype(BF16), v_ref[0, pl.ds(k0, tq), :])
            ls = ls_ref[...]
            for t in range(n_col):
                ls = ls + p[:, t * LANES:(t + 1) * LANES]
            ls_ref[...] = ls
            return c

        lax.fori_loop(0, qi + 1, weighted_sum, 0, unroll=min(qi + 1, ATTN_UNROLL))

        out = acc_ref[...] / jnp.sum(ls_ref[...], axis=1, keepdims=True)
        o_ref[0, pl.ds(q0, tq), :] = jnp.where(first, out[:tq], out[tq:]).astype(o_ref.dtype)
        return carry

    for qi in range(q_ref.shape[1] // tq):
        q_tile(qi, 0)


def _fox_attn(q, k, v, cum, q_gain, k_gain):
    b, l, d = q.shape
    tq = min(ATTN_TILE, l)
    pairs = d // LANES
    return pl.pallas_call(
        _fox_attn_kernel,
        grid=(b, pairs),
        in_specs=[
            pl.BlockSpec((1, l, LANES), lambda i, p: (i, 0, p)),
            pl.BlockSpec((1, l, LANES), lambda i, p: (i, 0, p)),
            pl.BlockSpec((1, l, LANES), lambda i, p: (i, 0, p)),
            pl.BlockSpec((1, 1, 2, l), lambda i, p: (i, p, 0, 0)),
            pl.BlockSpec((1, LANES), lambda i, p: (0, 0)),
            pl.BlockSpec((1, LANES), lambda i, p: (0, 0)),
        ],
        out_specs=pl.BlockSpec((1, l, LANES), lambda i, p: (i, 0, p)),
        out_shape=jax.ShapeDtypeStruct((b, l, d), BF16),
        scratch_shapes=[
            pltpu.VMEM((l, LANES), BF16),
            pltpu.VMEM((l // tq, 2 * tq, tq), F32),
            pltpu.VMEM((2 * tq, LANES), F32),
            pltpu.VMEM((2 * tq, LANES), F32),
            pltpu.VMEM((2 * tq, LANES), F32),
        ],
        compiler_params=_params(("arbitrary", "arbitrary")),
        name="fox_attn",
    )(q, k, v, cum, q_gain, k_gain)


def _outproj_kernel(head_norm, o_ref, gate_ref, gain_ref, w_ref, x_ref, g1_ref, sc_ref, sh_ref,
                    wr_ref, br_ref, x1_ref, h2_ref, meta_ref, wt_ref, counts_ref):
    tm = x_ref.shape[1]

    @pl.when((pl.program_id(0) == 0) & (pl.program_id(1) == 0))
    def _():
        counts_ref[...] = jnp.zeros(counts_ref.shape, F32)

    o = o_ref[0].astype(F32)
    gate = gate_ref[0].astype(F32)
    if head_norm:
        parts = []
        for hh in range(N_HEADS_GDN):
            seg = o[:, hh * HEAD_DIM_GDN:(hh + 1) * HEAD_DIM_GDN]
            parts.append(seg * lax.rsqrt(jnp.mean(seg * seg, axis=-1, keepdims=True) + EPS))
        o = jnp.concatenate(parts, axis=1) * gain_ref[...]
    y = _dot((o * gate).astype(BF16), w_ref[...])
    x1 = x_ref[0] + g1_ref[0] * y
    x1_ref[0] = x1
    h2 = _rms_mod(x1, sc_ref[0], sh_ref[0])
    h2_ref[0] = _pack_bf16_pairs(h2)
    lg = _dot_split_weights(h2, wr_ref[...], LANES) + br_ref[...]
    row = lax.broadcasted_iota(jnp.int32, (tm, tm), 0)
    col = lax.broadcasted_iota(jnp.int32, (tm, tm), 1)
    meta, wt, counts = _route_tile(lg, (row > col).astype(BF16), counts_ref[...])
    meta_ref[0] = meta.T[:SUBLANES, :]
    wt_ref[0] = wt
    counts_ref[...] = counts


def _outproj(head_norm, o, gate, gain, w_out, x, g1, sc2, sh2, w_route, b_route):
    b, l, d = x.shape
    tm = min(ROW_TILE, l)
    row = lambda i, j: (i, j, 0)
    vec = lambda i, j: (i, 0, 0)
    const = lambda i, j: (0, 0)
    return pl.pallas_call(
        functools.partial(_outproj_kernel, head_norm),
        grid=(b, l // tm),
        in_specs=[
            pl.BlockSpec((1, tm, d), row),
            pl.BlockSpec((1, tm, d), row),
            pl.BlockSpec(gain.shape, const),
            pl.BlockSpec(w_out.shape, const),
            pl.BlockSpec((1, tm, d), row),
            pl.BlockSpec((1, 1, d), vec),
            pl.BlockSpec((1, 1, d), vec),
            pl.BlockSpec((1, 1, d), vec),
            pl.BlockSpec(w_route.shape, const),
            pl.BlockSpec(b_route.shape, const),
        ],
        out_specs=[pl.BlockSpec((1, tm, d), row), pl.BlockSpec((1, tm, d // 2), row),
                   pl.BlockSpec((1, SUBLANES, tm), lambda i, j: (i, 0, j)), pl.BlockSpec((1, tm, LANES), row),
                   pl.BlockSpec((1, LANES), const)],
        out_shape=[jax.ShapeDtypeStruct((b, l, d), F32), jax.ShapeDtypeStruct((b, l, d // 2), PACKED),
                   jax.ShapeDtypeStruct((b, SUBLANES, l), jnp.int32), jax.ShapeDtypeStruct((b, l, LANES), F32),
                   jax.ShapeDtypeStruct((1, LANES), F32)],
        compiler_params=_params(("arbitrary", "arbitrary")),
        name="outproj",
    )(o, gate, gain, w_out, x, g1, sc2, sh2, w_route, b_route)


def _route_tile(lg, tri_strict, counts):
    lane_i = lax.broadcasted_iota(jnp.int32, lg.shape, 1)
    lane = lane_i.astype(F32)
    big = 1e9

    def first_argmax(v, vmax):
        return jnp.min(jnp.where(v == vmax, lane, big), axis=1, keepdims=True)

    is_group = lane < N_GROUPS
    gl = jnp.where(is_group, lg, NEG_BIG)
    g_max = jnp.max(gl, axis=1, keepdims=True)
    g_sum = jnp.sum(jnp.where(is_group, jnp.exp(gl - g_max), 0.0), axis=1, keepdims=True)
    g_p = 1.0 / g_sum
    g_idx = first_argmax(gl, g_max)
    lo = N_GROUPS + EXPERTS_PER_GROUP * g_idx
    el = jnp.where((lane >= lo) & (lane < lo + EXPERTS_PER_GROUP), lg, NEG_BIG)
    m1 = jnp.max(el, axis=1, keepdims=True)
    i1 = first_argmax(el, m1)
    el2 = jnp.where(lane == i1, NEG_BIG, el)
    m2 = jnp.max(el2, axis=1, keepdims=True)
    i2 = first_argmax(el2, m2)
    e = jnp.exp(m2 - m1)
    w1 = g_p / (1.0 + e)
    w2 = w1 * e
    e1 = i1 - N_GROUPS
    e2 = i2 - N_GROUPS
    hot1 = (lane == e1).astype(F32)
    hot2 = (lane == e2).astype(F32)
    hot = hot1 + hot2
    before = _dot(tri_strict, hot.astype(BF16)) + counts
    rank1 = jnp.sum(before * hot1, axis=1, keepdims=True)
    rank2 = jnp.sum(before * hot2, axis=1, keepdims=True)
    meta = jnp.where(lane_i == 0, e1, jnp.where(lane_i == 1, e2, jnp.where(lane_i == 2, rank1, rank2)))
    return (meta.astype(jnp.int32), jnp.where(lane_i == 0, w1, w2),
            counts + jnp.sum(hot, axis=0, keepdims=True))


def _index_windows(idx):
    n = idx.shape[0]
    return jnp.zeros((n // SC_WINDOW, LANES), jnp.int32).at[:, :SC_WINDOW].set(
        idx.reshape(n // SC_WINDOW, SC_WINDOW))


def _sc_mesh():
    return plsc.VectorSubcoreMesh(core_axis_name="core", subcore_axis_name="subcore")


def _scatter_rows(x, idx, n_rows):
    n_in, d = x.shape
    n_idx = idx.shape[0]

    def program(x_hbm, i_hbm, o_hbm):
        def window(x_vmem, i_vmem):
            pltpu.sync_copy(x_vmem, o_hbm.at[i_vmem.at[0, pl.ds(0, SC_WINDOW)]])

        pltpu.emit_pipeline(
            window,
            grid=(n_idx // SC_WINDOW,),
            in_specs=[pl.BlockSpec((SC_WINDOW, d), lambda i: (i % (n_in // SC_WINDOW), 0)),
                      pl.BlockSpec((1, LANES), lambda i: (i, 0))],
            out_specs=[],
            core_axis_name=("core", "subcore"),
            dimension_semantics=(pltpu.PARALLEL,),
        )(x_hbm, i_hbm)

    return pl.kernel(program, out_type=jax.ShapeDtypeStruct((n_rows, d), x.dtype), mesh=_sc_mesh(),
                     name="moe_scatter_rows")(x, _index_windows(idx))


def _gather_rows(x, idx):
    d = x.shape[1]
    n_idx = idx.shape[0]

    def program(x_hbm, i_hbm, o_hbm):
        def window(i_vmem, o_vmem):
            pltpu.sync_copy(x_hbm.at[i_vmem.at[0, pl.ds(0, SC_WINDOW)]], o_vmem)

        pltpu.emit_pipeline(
            window,
            grid=(n_idx // SC_WINDOW,),
            in_specs=[pl.BlockSpec((1, LANES), lambda i: (i, 0))],
            out_specs=[pl.BlockSpec((SC_WINDOW, d), lambda i: (i, 0))],
            core_axis_name=("core", "subcore"),
            dimension_semantics=(pltpu.PARALLEL,),
        )(i_hbm, o_hbm)

    return pl.kernel(program, out_type=jax.ShapeDtypeStruct((n_idx, d), x.dtype), mesh=_sc_mesh(),
                     name="moe_gather_rows")(x, _index_windows(idx))


def _gmm_kernel(first_ref, count_ref, xs_ref, wgu_ref, wd_ref, ys_ref, wgu_bf, wd_bf, xbuf, obuf, sem_in, sem_out):
    e = pl.program_id(0)
    last = pl.num_programs(0) - 1
    slots = xbuf.shape[0]
    tm = xbuf.shape[1]
    half = xbuf.shape[2]
    f = wd_bf.shape[0]
    n_tiles = count_ref[e]
    first = first_ref[e]
    total = first_ref[last] + count_ref[last]

    def rows(g):
        return pl.ds(pl.multiple_of(g * tm, tm), tm)

    def load(g):
        return pltpu.make_async_copy(xs_ref.at[rows(g)], xbuf.at[g % slots], sem_in.at[g % slots])

    def store(g):
        return pltpu.make_async_copy(obuf.at[g % slots], ys_ref.at[rows(g)], sem_out.at[g % slots])

    @pl.when(e == 0)
    def _():
        for g in range(slots - 1):
            @pl.when(g < total)
            def _():
                load(g).start()

    @pl.when(n_tiles > 0)
    def _():
        wgu_bf[...] = wgu_ref[0].astype(BF16)
        wd_bf[...] = wd_ref[0].astype(BF16)

    def tile(t, carry):
        g = first + t
        load(g).wait()

        @pl.when(g + slots - 1 < total)
        def _():
            load(g + slots - 1).start()

        @pl.when(g >= slots)
        def _():
            store(g - slots).wait()

        x_hi, x_lo = _unpack_bf16_pairs(xbuf[g % slots])
        hu = _dot(x_hi.astype(BF16), wgu_bf[:half, :]) + _dot(x_lo.astype(BF16), wgu_bf[half:, :])
        act = _silu(hu[:, :f]) * hu[:, f:]
        obuf[g % slots] = _pack_bf16_pairs(_dot(act.astype(BF16), wd_bf[...]))
        store(g).start()
        return carry

    lax.fori_loop(0, n_tiles, tile, 0)

    @pl.when(e == last)
    def _():
        for j in range(slots):
            g = total - slots + j

            @pl.when(g >= 0)
            def _():
                store(g).wait()


def _gmm(layer, tile_first, tile_count, xs, w_gate_up, w_down):
    n_rows, half = xs.shape
    d = 2 * half
    tm = MOE_TILE
    f2 = w_gate_up.shape[2]
    grid_spec = pltpu.PrefetchScalarGridSpec(
        num_scalar_prefetch=2,
        grid=(N_EXPERTS,),
        in_specs=[
            pl.BlockSpec(memory_space=pl.ANY),
            pl.BlockSpec((1, d, f2), lambda e, tf, tc: (layer * N_EXPERTS + e, 0, 0)),
            pl.BlockSpec((1, f2 // 2, d), lambda e, tf, tc: (layer * N_EXPERTS + e, 0, 0)),
        ],
        out_specs=pl.BlockSpec(memory_space=pl.ANY),
        scratch_shapes=[
            pltpu.VMEM((d, f2), BF16), pltpu.VMEM((f2 // 2, d), BF16),
            pltpu.VMEM((GMM_SLOTS, tm, half), PACKED), pltpu.VMEM((GMM_SLOTS, tm, half), PACKED),
            pltpu.SemaphoreType.DMA((GMM_SLOTS,)), pltpu.SemaphoreType.DMA((GMM_SLOTS,)),
        ],
    )
    return pl.pallas_call(
        _gmm_kernel,
        grid_spec=grid_spec,
        out_shape=jax.ShapeDtypeStruct((n_rows, half), PACKED),
        input_output_aliases={2: 0},
        compiler_params=_params(("arbitrary",)),
        name="moe_gmm",
    )(tile_first, tile_count, xs, w_gate_up, w_down)


def _combine_kernel(y0_ref, y1_ref, wt_ref, x_ref, g2_ref, fg_ref, o_ref):
    x2 = x_ref[...] + g2_ref[0] * _combine_rows(y0_ref[...], y1_ref[...], wt_ref[...])
    o_ref[...] = x2 * lax.rsqrt(jnp.mean(x2 * x2, axis=-1, keepdims=True) + EPS) * fg_ref[...]


def _combine_final(yg, wt, x1, g2, final_gain, tokens_per_seq):
    t, d = x1.shape
    tt = min(TOKEN_TILE, tokens_per_seq)
    nb = t // tt
    per_seq = tokens_per_seq // tt
    return pl.pallas_call(
        _combine_kernel,
        grid=(nb,),
        in_specs=[
            pl.BlockSpec((tt, d // 2), lambda i: (i, 0)),
            pl.BlockSpec((tt, d // 2), lambda i: (i + nb, 0)),
            pl.BlockSpec((tt, LANES), lambda i: (i, 0)),
            pl.BlockSpec((tt, d), lambda i: (i, 0)),
            pl.BlockSpec((1, 1, d), lambda i: (i // per_seq, 0, 0)),
            pl.BlockSpec((1, d), lambda i: (0, 0)),
        ],
        out_specs=pl.BlockSpec((tt, d), lambda i: (i, 0)),
        out_shape=jax.ShapeDtypeStruct((t, d), F32),
        compiler_params=_params(("arbitrary",)),
        name="moe_combine",
    )(yg, yg, wt, x1, g2, final_gain)


def _sorted_positions(meta, counts, tile):
    counts = counts[0, :N_EXPERTS].astype(jnp.int32)
    tiles_per = (counts + tile - 1) // tile
    tile_first = jnp.cumsum(tiles_per) - tiles_per
    choice_major = lambda a: jnp.transpose(a, (1, 0, 2)).reshape(TOP_K, -1)
    eid = choice_major(meta[:, :TOP_K, :])
    rank = choice_major(meta[:, TOP_K:2 * TOP_K, :])
    onehot = (eid[..., None] == jnp.arange(N_EXPERTS, dtype=jnp.int32)).astype(jnp.int32)
    pos = jnp.sum(onehot * (tile_first * tile), axis=-1) + rank
    return pos.astype(jnp.int32), tile_first.astype(jnp.int32), tiles_per.astype(jnp.int32)


def _sorted_rows(n_tokens, tile):
    return (TOP_K * n_tokens // tile + N_EXPERTS) * tile


def _hier_moe(layer, h2, meta, counts, w_gate_up, w_down):
    pos, tile_first, tile_count = _sorted_positions(meta, counts, MOE_TILE)
    pos = pos.reshape(-1)
    xs = _scatter_rows(h2, pos, _sorted_rows(h2.shape[0], MOE_TILE))
    ys = _gmm(layer, tile_first, tile_count, xs, w_gate_up, w_down)
    return _gather_rows(ys, pos)


def _hi_lo_cols(w, n):
    k, m = w.shape
    wp = jnp.zeros((k, n), F32).at[:, :m].set(w)
    hi = wp.astype(BF16)
    lo = (wp - hi.astype(F32)).astype(BF16)
    return jnp.concatenate([hi, lo], axis=1)


def _pad_row(v, n):
    return jnp.zeros((1, n), F32).at[0, :v.shape[0]].set(v)


def kernel(x, c, ada_w, ada_b, gdn_w_in, gdn_conv, gdn_a_log, gdn_dt_bias, gdn_norm, gdn_w_out,
           fox_w_in, fox_b_f, fox_q_norm, fox_k_norm, fox_w_out,
           moe_w_group, moe_b_group, moe_w_router, moe_b_router, moe_w_gate_up, moe_w_down,
           final_norm):
    b, l, d = x.shape
    depth = ada_w.shape[0]
    t = b * l
    ada = _ada(c, ada_w, ada_b)
    w_gate_up = moe_w_gate_up.reshape(depth * N_EXPERTS, d, moe_w_gate_up.shape[-1])
    w_down = moe_w_down.reshape(depth * N_EXPERTS, moe_w_down.shape[-2], d)
    final_gain = final_norm.reshape(1, d)

    x_src = x
    for i in range(depth):
        sh1, sc1, g1, sh2, sc2, g2 = [ada[i, :, s * d:(s + 1) * d].reshape(b, 1, d) for s in range(6)]
        j = i // 2
        w_route = _hi_lo_cols(jnp.concatenate([moe_w_group[i], moe_w_router[i]], axis=1), LANES)
        b_route = _pad_row(jnp.concatenate([moe_b_group[i], moe_b_router[i]]), LANES)
        if i % 2 == 0:
            w_in = gdn_w_in[j]
            q, k, v, gate, gb, *x_new = _gdn_inproj(
                x_src, sc1, sh1, w_in.astype(BF16), _hi_lo_cols(w_in[:, 4 * d:], LANES),
                gdn_conv[j], jnp.concatenate([_pad_row(gdn_a_log[j], LANES), _pad_row(gdn_dt_bias[j], LANES)]))
            x = x_new[0] if x_new else x_src
            o = _gdn_chunk(q, k, v, gb)
            gain = jnp.tile(gdn_norm[j], N_HEADS_GDN).reshape(1, d)
            x1, h2, meta, wt, counts = _outproj(True, o, gate, gain, gdn_w_out[j].astype(BF16), x, g1, sc2, sh2,
                                      w_route, b_route)
        else:
            w_in = fox_w_in[j]
            q, k, v, gate, cum_t, *x_new = _fox_inproj(
                x_src, sc1, sh1, w_in.astype(BF16), _hi_lo_cols(w_in[:, 4 * d:], LANES),
                _pad_row(fox_b_f[j], LANES))
            x = x_new[0] if x_new else x_src
            cum =cum_t[:, :N_HEADS_FOX, :].reshape(b, N_HEADS_FOX // 2, 2, l)
            q_gain = jnp.tile(fox_q_norm[j], 2).reshape(1, LANES)
            k_gain = jnp.tile(fox_k_norm[j], 2).reshape(1, LANES)
            o = _fox_attn(q, k, v, cum, q_gain, k_gain)
            x1, h2, meta, wt, counts = _outproj(False, o, gate, final_gain, fox_w_out[j].astype(BF16), x, g1, sc2, sh2,
                                      w_route, b_route)
        yg = _hier_moe(i, h2.reshape(t, d // 2), meta, counts, w_gate_up, w_down)
        x_src = (x1, yg, wt.reshape(t, LANES), g2)
    x1, yg, wt, g2 = x_src
    return _combine_final(yg, wt, x1.reshape(t, d), g2, final_gain, l).reshape(b, l, d)
```

```python
import functools

import jax
import jax.numpy as jnp
from jax import lax
from jax.experimental import pallas as pl
from jax.experimental.pallas import tpu as pltpu
from jax.experimental.pallas import tpu_sc as plsc

F32 = jnp.float32
BF16 = jnp.bfloat16
PACKED = jnp.int32

EPS = 1e-6
CHUNK = 64
N_HEADS_GDN = 8
HEAD_DIM_GDN = 128
N_HEADS_FOX = 16
HEAD_DIM_FOX = 64
CONV_K = 4
N_GROUPS = 4
EXPERTS_PER_GROUP = 8
N_EXPERTS = N_GROUPS * EXPERTS_PER_GROUP
TOP_K = 2

LANES = 128
SUBLANES = 8
VMEM_LIMIT_BYTES = 56 * 1024 * 1024

ROW_TILE = 512
ATTN_TILE = 512
ATTN_UNROLL = 2
GDN_BLOCK = 1024
GDN_CHUNKS_PER_ITER = 4
GDN_HEADS_PACKED = 4
MOE_TILE = 256
GMM_SLOTS = 4
TOKEN_TILE = 512
SC_WINDOW = 64
NEG_BIG = -1e30
LOG2E = 1.4426950408889634


def _params(semantics):
    return pltpu.CompilerParams(dimension_semantics=semantics, vmem_limit_bytes=VMEM_LIMIT_BYTES)


def _sigmoid(x):
    return 0.5 * jnp.tanh(0.5 * x) + 0.5


def _silu(x):
    return x * _sigmoid(x)


def _softplus(x):
    return jnp.maximum(x, 0.0) + jnp.log(1.0 + jnp.exp(-jnp.abs(x)))


def _split2(a):
    hi = a.astype(BF16)
    lo = (a - hi.astype(F32)).astype(BF16)
    return hi, lo


def _split3(a):
    hi = a.astype(BF16)
    r = a - hi.astype(F32)
    mid = r.astype(BF16)
    lo = (r - mid.astype(F32)).astype(BF16)
    return hi, mid, lo


def _dot(a, b):
    return jnp.dot(a, b, preferred_element_type=F32)


def _dot_nt(a, b):
    return lax.dot_general(a, b, (((1,), (1,)), ((), ())), preferred_element_type=F32)


def _dot_split_weights(a, w_hi_lo, n):
    a_hi, a_lo = _split2(a)
    r = _dot(a_hi, w_hi_lo)
    return r[:, :n] + r[:, n:] + _dot(a_lo, w_hi_lo[:, :n])


def _cumsum_rows(tri_bf16, v):
    hi, mid, lo = _split3(v)
    r = _dot(tri_bf16, jnp.concatenate([hi, mid], axis=1))
    return r[:, :LANES] + r[:, LANES:] + _dot(tri_bf16, lo)


def _rms_mod(x, sc, sh):
    ms = jnp.mean(x * x, axis=-1, keepdims=True)
    return x * lax.rsqrt(ms + EPS) * (1.0 + sc) + sh


def _pack_bf16_pairs(a):
    n = a.shape[1] // 2
    bits = lax.bitcast_convert_type(a.astype(BF16).astype(F32), jnp.uint32)
    words = (bits[:, :n] & jnp.uint32(0xFFFF0000)) | (bits[:, n:] >> 16)
    return lax.bitcast_convert_type(words, PACKED)


def _unpack_bf16_pairs(p):
    words = lax.bitcast_convert_type(p, jnp.uint32)
    hi = lax.bitcast_convert_type(words & jnp.uint32(0xFFFF0000), F32)
    lo = lax.bitcast_convert_type(words << 16, F32)
    return hi, lo


def _lane_col(v, idx, lane):
    return jnp.sum(jnp.where(lane == idx, v, 0.0), axis=1, keepdims=True)


def _ada_kernel(c_ref, w_ref, b_ref, o_ref):
    c = c_ref[...]
    o_ref[0] = _dot(_silu(c).astype(BF16), w_ref[0].astype(BF16)) + b_ref[0]


def _ada(c, ada_w, ada_b):
    depth, d, n = ada_w.shape
    b = c.shape[0]
    tn = 1536
    return pl.pallas_call(
        _ada_kernel,
        grid=(depth, n // tn),
        in_specs=[
            pl.BlockSpec((b, d), lambda i, j: (0, 0)),
            pl.BlockSpec((1, d, tn), lambda i, j: (i, 0, j)),
            pl.BlockSpec((1, 1, tn), lambda i, j: (i, 0, j)),
        ],
        out_specs=pl.BlockSpec((1, b, tn), lambda i, j: (i, 0, j)),
        out_shape=jax.ShapeDtypeStruct((depth, b, n), F32),
        compiler_params=_params(("arbitrary", "arbitrary")),
        name="ada",
    )(c, ada_w, ada_b.reshape(depth, 1, n))


def _combine_rows(y0, y1, wt):
    lane = lax.broadcasted_iota(jnp.int32, wt.shape, 1)
    w0 = _lane_col(wt, 0, lane)
    w1 = _lane_col(wt, 1, lane)
    a_hi, a_lo = _unpack_bf16_pairs(y0)
    b_hi, b_lo = _unpack_bf16_pairs(y1)
    return jnp.concatenate([w0 * a_hi + w1 * b_hi, w0 * a_lo + w1 * b_lo], axis=1)


N_PENDING_REFS = 5


def _residual_tile(pending, refs):
    if not pending:
        return refs[0][0]
    x1_ref, y0_ref, y1_ref, wt_ref, g2_ref = refs
    return x1_ref[0] + g2_ref[0] * _combine_rows(y0_ref[...], y1_ref[...], wt_ref[...])


def _gdn_inproj_kernel(pending, *refs):
    nx = N_PENDING_REFS if pending else 1
    sc_ref, sh_ref, w_ref, wab_ref, conv_ref, prm_ref, q_ref, k_ref, v_ref, gate_ref, gb_ref = refs[nx:nx + 11]
    halo_ref = refs[-1]
    x = _residual_tile(pending, refs[:nx])
    if pending:
        refs[nx + 11][0] = x
    tm, d = x.shape
    h = _rms_mod(x, sc_ref[0], sh_ref[0])
    hb = h.astype(BF16)

    @pl.when(pl.program_id(1) == 0)
    def _():
        halo_ref[:, 0:SUBLANES, :] = jnp.zeros((3, SUBLANES, d), F32)

    for s, o_ref in enumerate((q_ref, k_ref, v_ref)):
        raw = _dot(hb, w_ref[:, s * d:(s + 1) * d])
        halo_ref[s, SUBLANES:SUBLANES + tm, :] = raw
        cw = conv_ref[:, s * d:(s + 1) * d]
        y = raw * cw[3:4]
        for j in range(CONV_K - 1):
            off = SUBLANES - (CONV_K - 1) + j
            y = y + halo_ref[s, off:off + tm, :] * cw[j:j + 1]
        halo_ref[s, 0:SUBLANES, :] = halo_ref[s, tm:tm + SUBLANES, :]
        y = _silu(y)
        if s < 2:
            scale = HEAD_DIM_GDN ** -0.5 if s == 0 else 1.0
            for hh in range(N_HEADS_GDN):
                seg = y[:, hh * HEAD_DIM_GDN:(hh + 1) * HEAD_DIM_GDN]
                inv = lax.rsqrt(jnp.sum(seg * seg, axis=-1, keepdims=True) + EPS) * scale
                o_ref[0, :, hh * HEAD_DIM_GDN:(hh + 1) * HEAD_DIM_GDN] = (seg * inv).astype(o_ref.dtype)
        else:
            o_ref[0] = y.astype(o_ref.dtype)

    gate_ref[0] = _silu(_dot(hb, w_ref[:, 3 * d:4 * d])).astype(gate_ref.dtype)

    ab = _dot_split_weights(h, wab_ref[...], LANES)
    log_alpha = -jnp.exp(prm_ref[0:1, :]) * _softplus(ab + prm_ref[1:2, :])
    lane = lax.broadcasted_iota(jnp.int32, ab.shape, 1)
    gb_ref[0] = jnp.where(lane < N_HEADS_GDN, log_alpha, _sigmoid(ab))


def _residual_operands(x_src, tm):
    row = lambda i, j: (i, j, 0)
    if not isinstance(x_src, tuple):
        b, l, d = x_src.shape
        return False, [x_src], [pl.BlockSpec((1, tm, d), row)], [], []
    x1, yg, wt, g2 = x_src
    b, l, d = x1.shape
    per_seq = l // tm
    n_tiles = b * per_seq
    specs = [
        pl.BlockSpec((1, tm, d), row),
        pl.BlockSpec((tm, d // 2), lambda i, j: (i * per_seq + j, 0)),
        pl.BlockSpec((tm, d // 2), lambda i, j: (i * per_seq + j + n_tiles, 0)),
        pl.BlockSpec((tm, LANES), lambda i, j: (i * per_seq + j, 0)),
        pl.BlockSpec((1, 1, d), lambda i, j: (i, 0, 0)),
    ]
    return (True, [x1, yg, yg, wt, g2], specs, [jax.ShapeDtypeStruct((b, l, d), F32)],
            [pl.BlockSpec((1, tm, d), row)])


def _gdn_inproj(x_src, sc, sh, w_main, w_ab, conv_w, prm):
    b, l, d = (x_src[0] if isinstance(x_src, tuple) else x_src).shape
    tm = min(ROW_TILE, l)
    pending, x_ops, x_specs, x_out_shape, x_out_spec = _residual_operands(x_src, tm)
    act = jax.ShapeDtypeStruct((b, l, d), BF16)
    row = lambda i, j: (i, j, 0)
    vec = lambda i, j: (i, 0, 0)
    const = lambda i, j: (0, 0)
    return pl.pallas_call(
        functools.partial(_gdn_inproj_kernel, pending),
        grid=(b, l // tm),
        in_specs=x_specs + [
            pl.BlockSpec((1, 1, d), vec),
            pl.BlockSpec((1, 1, d), vec),
            pl.BlockSpec(w_main.shape, const),
            pl.BlockSpec(w_ab.shape, const),
            pl.BlockSpec(conv_w.shape, const),
            pl.BlockSpec(prm.shape, const),
        ],
        out_specs=[pl.BlockSpec((1, tm, d), row)] * 4 + [pl.BlockSpec((1, tm, LANES), row)] + x_out_spec,
        out_shape=[act, act, act, act, jax.ShapeDtypeStruct((b, l, LANES), F32)] + x_out_shape,
        scratch_shapes=[pltpu.VMEM((3, tm + SUBLANES, d), F32)],
        compiler_params=_params(("arbitrary", "arbitrary")),
        name="gdn_inproj",
    )(*x_ops, sc, sh, w_main, w_ab, conv_w, prm)


def _block_diag(r, blk, n_blk):
    return jnp.concatenate([jnp.where(blk == b, r, jnp.zeros_like(r)) for b in range(n_blk)], axis=0)


def _mm_packed(lhs, rhs, blk, n_blk):
    m = lhs.shape[0]
    l_hi, l_lo = _split2(lhs)
    r_hi, r_lo = _split2(rhs)
    t = _dot(jnp.concatenate([l_hi, l_lo], axis=0), _block_diag(r_hi, blk, n_blk))
    return t[:m] + t[m:] + _dot(l_hi, _block_diag(r_lo, blk, n_blk))


def _unit_lower_inverse(a_strict, eye, blk, n_blk):
    n = a_strict[0].shape[0]
    levels = n.bit_length() - 1
    m = [-a for a in a_strict]
    x = [eye + mi for mi in m]
    p = [_mm_packed(mi, mi, blk, n_blk) for mi in m]
    for _ in range(levels - 2):
        r = [_mm_packed(jnp.concatenate([pi, xi], axis=0), pi, blk, n_blk) for pi, xi in zip(p, x)]
        x = [xi + ri[n:] for xi, ri in zip(x, r)]
        p = [ri[:n] for ri in r]
    return [xi + _mm_packed(xi, pi, blk, n_blk) for xi, pi in zip(x, p)]


def _gdn_chunk_kernel(q_ref, k_ref, v_ref, gb_ref, o_ref, s_ref):
    lb = q_ref.shape[1]
    c = CHUNK
    dk = HEAD_DIM_GDN
    nc = GDN_CHUNKS_PER_ITER
    heads = range(N_HEADS_GDN)
    sl = [slice(hh * dk, (hh + 1) * dk) for hh in heads]

    @pl.when(pl.program_id(1) == 0)
    def _():
        s_ref[...] = jnp.zeros(s_ref.shape, F32)

    pk = GDN_HEADS_PACKED
    groups = N_HEADS_GDN // pk
    tri = (lax.broadcasted_iota(jnp.int32, (c, c), 0) >= lax.broadcasted_iota(jnp.int32, (c, c), 1)).astype(BF16)
    lane = lax.broadcasted_iota(jnp.int32, (c, LANES), 1)
    side = lane // c
    row_p = lax.broadcasted_iota(jnp.int32, (c, pk * c), 0)
    lane_p = lax.broadcasted_iota(jnp.int32, (c, pk * c), 1)
    blk = lane_p // c
    col_p = lane_p - blk * c
    incl = row_p >= col_p
    strict = row_p > col_p
    diag = row_p == col_p
    eye = diag.astype(F32)
    blk_k = lax.broadcasted_iota(jnp.int32, (c, pk * dk), 1) // dk

    def per_head(cols, width):
        return jnp.concatenate([jnp.broadcast_to(col, (c, width)) for col in cols], axis=1)

    def body(it, carry):
        r0, q, kd_t, egl_last, rhs_pair, kq, decay, eg_cols = [], [], [], [], [], [], [], []
        for j in range(nc):
            r = pl.multiple_of((it * nc + j) * c, c)
            r0.append(r)
            gb = gb_ref[0, pl.ds(r, c), :]
            g = _cumsum_rows(tri, gb)
            g_last = g[c - 1:c, :]
            eg = jnp.exp(g)
            egl = jnp.exp(g_last - g)
            eg_last = jnp.exp(g_last)
            for gi in range(groups):
                hs = [gi * pk + b for b in range(pk)]
                cols = slice(gi * pk * dk, (gi + 1) * pk * dk)
                q4 = q_ref[0, pl.ds(r, c), cols].astype(F32)
                k4 = k_ref[0, pl.ds(r, c), cols].astype(F32)
                v4 = v_ref[0, pl.ds(r, c), cols].astype(F32)
                beta4 = per_head([_lane_col(gb, N_HEADS_GDN + hh, lane) for hh in hs], dk)
                eg4 = per_head([_lane_col(eg, hh, lane) for hh in hs], dk)
                egl4 = per_head([_lane_col(egl, hh, lane) for hh in hs], dk)
                kb4 = k4 * beta4
                k4b = k4.astype(BF16)
                k_diag = jnp.concatenate([jnp.where(blk_k == b, k4b, jnp.zeros_like(k4b)) for b in range(pk)],
                                         axis=0)
                kq.append(_dot_nt(jnp.concatenate([kb4, q4], axis=0).astype(BF16), k_diag))
                g_col = per_head([_lane_col(g, hh, lane) for hh in hs], c)
                g_row = jnp.sum(jnp.where(diag, g_col, 0.0), axis=0, keepdims=True)
                decay.append(jnp.where(incl, jnp.exp(jnp.where(incl, g_col - g_row, 0.0)), 0.0))
                vb4 = v4 * beta4
                kbe4 = kb4 * eg4
                qe4 = q4 * eg4
                kd4 = k4 * egl4
                for b, hh in enumerate(hs):
                    hsl = slice(b * dk, (b + 1) * dk)
                    q.append(qe4[:, hsl])
                    kd_t.append(kd4[:, hsl].T.astype(BF16))
                    egl_last.append(_lane_col(eg_last, hh, lane[0:1]))
                for p in range(pk // 2):
                    a = slice(2 * p * dk, (2 * p + 1) * dk)
                    bsl = slice((2 * p + 1) * dk, (2 * p + 2) * dk)
                    rhs_pair.append(jnp.concatenate(
                        [jnp.concatenate([vb4[:, a], kbe4[:, a]], axis=1),
                         jnp.concatenate([vb4[:, bsl], kbe4[:, bsl]], axis=1)], axis=0).astype(BF16))
        n_grp = len(kq)
        t_inv = _unit_lower_inverse([jnp.where(strict, kq[i][:c] * decay[i], 0.0) for i in range(n_grp)],
                                    eye, blk, pk)
        qk_p = [kq[i][c:] * decay[i] for i in range(n_grp)]

        def head_lhs(packed, i_grp, b):
            tile = packed[i_grp][:, (b // 2) * LANES:(b // 2 + 1) * LANES]
            return jnp.where(side == b % 2, tile, 0.0).astype(BF16)

        uw, qk = [], []
        for i_grp in range(n_grp):
            for b in range(pk):
                uw.append(_dot(head_lhs(t_inv, i_grp, b), rhs_pair[i_grp * (pk // 2) + b // 2]))
                qk.append(head_lhs(qk_p, i_grp, b))
        n = len(uw)
        wq = [jnp.concatenate([uw[i][:, dk:], q[i]], axis=0).astype(BF16) for i in range(n)]

        s = [s_ref[hh] for hh in heads]
        for j in range(nc):
            idx = [j * N_HEADS_GDN + hh for hh in heads]
            ws_qs = [_dot(wq[i], s[hh].astype(BF16)) for hh, i in zip(heads, idx)]
            v_new = [(uw[i][:, :dk] - ws_qs[hh][:c]).astype(BF16) for hh, i in zip(heads, idx)]
            for hh, i in zip(heads, idx):
                pair = jnp.concatenate([v_new[hh - hh % 2], v_new[hh - hh % 2 + 1]], axis=0)
                o = ws_qs[hh][c:] + _dot(qk[i], pair)
                o_ref[0, pl.ds(r0[j], c), sl[hh]] = o.astype(o_ref.dtype)
            s = [s[hh] * egl_last[i] + _dot(kd_t[i], v_new[hh]) for hh, i in zip(heads, idx)]
        for hh in heads:
            s_ref[hh] = s[hh]
        return carry

    lax.fori_loop(0, lb // (c * nc), body, 0)


def _gdn_chunk(q, k, v, gb):
    b, l, d = q.shape
    lb = min(GDN_BLOCK, l)
    row = lambda i, j: (i, j, 0)
    return pl.pallas_call(
        _gdn_chunk_kernel,
        grid=(b, l // lb),
        in_specs=[pl.BlockSpec((1, lb, d), row)] * 3 + [pl.BlockSpec((1, lb, LANES), row)],
        out_specs=pl.BlockSpec((1, lb, d), row),
        out_shape=jax.ShapeDtypeStruct((b, l, d), F32),
        scratch_shapes=[pltpu.VMEM((N_HEADS_GDN, HEAD_DIM_GDN, HEAD_DIM_GDN), F32)],
        compiler_params=_params(("arbitrary", "arbitrary")),
        name="gdn_chunk",
    )(q, k, v, gb)


def _fox_inproj_kernel(pending, *refs):
    nx = N_PENDING_REFS if pending else 1
    sc_ref, sh_ref, w_ref, wf_ref, bf_ref, tri_ref, q_ref, k_ref, v_ref, g_ref, cum_t_ref = refs[nx:nx + 11]
    carry_ref = refs[-1]
    x = _residual_tile(pending, refs[:nx])
    if pending:
        refs[nx + 11][0] = x
    tm, d = x.shape
    h = _rms_mod(x, sc_ref[0], sh_ref[0])
    hb = h.astype(BF16)
    for s, o_ref in enumerate((q_ref, k_ref, v_ref)):
        o_ref[0] = _dot(hb, w_ref[:, s * d:(s + 1) * d]).astype(o_ref.dtype)
    g_ref[0] = _sigmoid(_dot(hb, w_ref[:, 3 * d:4 * d])).astype(g_ref.dtype)

    @pl.when(pl.program_id(1) == 0)
    def _():
        carry_ref[...] = jnp.zeros(carry_ref.shape, F32)

    f_logit = _dot_split_weights(h, wf_ref[...], LANES) + bf_ref[...]
    log_f = -_softplus(-f_logit)
    cum = _cumsum_rows(tri_ref[...], log_f) + carry_ref[...]
    carry_ref[...] = cum[tm - 1:tm, :]
    cum_t_ref[0] = cum.T


def _fox_inproj(x_src, sc, sh, w_main, w_f, b_f):
    b, l, d = (x_src[0] if isinstance(x_src, tuple) else x_src).shape
    tm = min(ROW_TILE, l)
    pending, x_ops, x_specs, x_out_shape, x_out_spec = _residual_operands(x_src, tm)
    act = jax.ShapeDtypeStruct((b, l, d), BF16)
    row = lambda i, j: (i, j, 0)
    vec = lambda i, j: (i, 0, 0)
    const = lambda i, j: (0, 0)
    return pl.pallas_call(
        functools.partial(_fox_inproj_kernel, pending),
        grid=(b, l // tm),
        in_specs=x_specs + [
            pl.BlockSpec((1, 1, d), vec),
            pl.BlockSpec((1, 1, d), vec),
            pl.BlockSpec(w_main.shape, const),
            pl.BlockSpec(w_f.shape, const),
            pl.BlockSpec(b_f.shape, const),
            pl.BlockSpec((tm, tm), const),
        ],
        out_specs=([pl.BlockSpec((1, tm, d), row)] * 4 + [pl.BlockSpec((1, LANES, tm), lambda i, j: (i, 0, j))]
                   + x_out_spec),
        out_shape=[act, act, act, act, jax.ShapeDtypeStruct((b, LANES, l), F32)] + x_out_shape,
        scratch_shapes=[pltpu.VMEM((1, LANES), F32)],
        compiler_params=_params(("arbitrary", "arbitrary")),
        name="fox_inproj",
    )(*x_ops, sc, sh, w_main, w_f, b_f, jnp.tril(jnp.ones((tm, tm), BF16)))


def _fox_attn_kernel(q_ref, k_ref, v_ref, cum_ref, qg_ref, kg_ref, o_ref, kn_ref, s_ref, mx_ref, ls_ref, acc_ref):
    tq = s_ref.shape[2]
    hd = HEAD_DIM_FOX
    lane = lax.broadcasted_iota(jnp.int32, (1, LANES), 1)
    first = lane < hd
    n_col = tq // LANES

    def head_norm(t, gain):
        sq = t * t
        s0 = jnp.sum(jnp.where(first, sq, 0.0), axis=1, keepdims=True)
        s1 = jnp.sum(jnp.where(first, 0.0, sq), axis=1, keepdims=True)
        ms = jnp.where(first, s0, s1) * (1.0 / hd)
        return t * lax.rsqrt(ms + EPS) * gain

    kn_ref[...] = head_norm(k_ref[0].astype(F32), kg_ref[...]).astype(kn_ref.dtype)

    def q_tile(qi, carry):
        q0 = qi * tq
        qn = head_norm(q_ref[0, pl.ds(q0, tq), :].astype(F32), qg_ref[...]) * (hd ** -0.5 * LOG2E)
        q2 = jnp.concatenate([jnp.where(first, qn, 0.0), jnp.where(first, 0.0, qn)], axis=0).astype(BF16)

        def scores(j, masked):
            k0 = j * tq if isinstance(j, int) else pl.multiple_of(j * tq, tq)
            ck = cum_ref[0, 0, :, pl.ds(k0, tq)] * LOG2E
            s = _dot_nt(q2, kn_ref[pl.ds(k0, tq), :])
            s = jnp.concatenate([s[:tq] - ck[0:1], s[tq:] - ck[1:2]], axis=0)
            if masked:
                r = lax.broadcasted_iota(jnp.int32, (tq, tq), 0)
                c = lax.broadcasted_iota(jnp.int32, (tq, tq), 1)
                keep = jnp.concatenate([r >= c, r >= c], axis=0)
                s = jnp.where(keep, s, NEG_BIG)
            s_ref[j] = s
            mx = mx_ref[...]
            for t in range(n_col):
                mx = jnp.maximum(mx, s[:, t * LANES:(t + 1) * LANES])
            mx_ref[...] = mx

        def scores_step(j, c):
            scores(j, False)
            return c

        mx_ref[...] = jnp.full(mx_ref.shape, NEG_BIG, F32)
        if qi > 0:
            lax.fori_loop(0, qi, scores_step, 0, unroll=min(qi, ATTN_UNROLL))
        scores(qi, True)
        mx_ref[...] = jnp.broadcast_to(jnp.max(mx_ref[...], axis=1, keepdims=True), mx_ref.shape)

        acc_ref[...] = jnp.zeros(acc_ref.shape, F32)
        ls_ref[...] = jnp.zeros(ls_ref.shape, F32)

        def weighted_sum(j, c):
            k0 = j * tq if isinstance(j, int) else pl.multiple_of(j * tq, tq)
            m = mx_ref[...]
            s = s_ref[j]
            p = jnp.concatenate([jnp.exp2(s[:, t * LANES:(t + 1) * LANES] - m) for t in range(n_col)], axis=1)
            acc_ref[...] += _dot(p.astype(BF16), v_ref[0, pl.ds(k0, tq), :])
            ls = ls_ref[...]
            for t in range(n_col):
                ls = ls + p[:, t * LANES:(t + 1) * LANES]
            ls_ref[...] = ls
            return c

        lax.fori_loop(0, qi + 1, weighted_sum, 0, unroll=min(qi + 1, ATTN_UNROLL))

        out = acc_ref[...] / jnp.sum(ls_ref[...], axis=1, keepdims=True)
        o_ref[0, pl.ds(q0, tq), :] = jnp.where(first, out[:tq], out[tq:]).astype(o_ref.dtype)
        return carry

    for qi in range(q_ref.shape[1] // tq):
        q_tile(qi, 0)


def _fox_attn(q, k, v, cum, q_gain, k_gain):
    b, l, d = q.shape
    tq = min(ATTN_TILE, l)
    pairs = d // LANES
    return pl.pallas_call(
        _fox_attn_kernel,
        grid=(b, pairs),
        in_specs=[
            pl.BlockSpec((1, l, LANES), lambda i, p: (i, 0, p)),
            pl.BlockSpec((1, l, LANES), lambda i, p: (i, 0, p)),
            pl.BlockSpec((1, l, LANES), lambda i, p: (i, 0, p)),
            pl.BlockSpec((1, 1, 2, l), lambda i, p: (i, p, 0, 0)),
            pl.BlockSpec((1, LANES), lambda i, p: (0, 0)),
            pl.BlockSpec((1, LANES), lambda i, p: (0, 0)),
        ],
        out_specs=pl.BlockSpec((1, l, LANES), lambda i, p: (i, 0, p)),
        out_shape=jax.ShapeDtypeStruct((b, l, d), BF16),
        scratch_shapes=[
            pltpu.VMEM((l, LANES), BF16),
            pltpu.VMEM((l // tq, 2 * tq, tq), F32),
            pltpu.VMEM((2 * tq, LANES), F32),
            pltpu.VMEM((2 * tq, LANES), F32),
            pltpu.VMEM((2 * tq, LANES), F32),
        ],
        compiler_params=_params(("arbitrary", "arbitrary")),
        name="fox_attn",
    )(q, k, v, cum, q_gain, k_gain)


def _outproj_kernel(head_norm, o_ref, gate_ref, gain_ref, w_ref, x_ref, g1_ref, sc_ref, sh_ref,
                    wr_ref, br_ref, tri_ref, x1_ref, h2_ref, meta_ref, wt_ref, counts_ref):

    @pl.when((pl.program_id(0) == 0) & (pl.program_id(1) == 0))
    def _():
        counts_ref[...] = jnp.zeros(counts_ref.shape, F32)

    o = o_ref[0].astype(F32)
    gate = gate_ref[0].astype(F32)
    if head_norm:
        parts = []
        for hh in range(N_HEADS_GDN):
            seg = o[:, hh * HEAD_DIM_GDN:(hh + 1) * HEAD_DIM_GDN]
            parts.append(seg * lax.rsqrt(jnp.mean(seg * seg, axis=-1, keepdims=True) + EPS))
        o = jnp.concatenate(parts, axis=1) * gain_ref[...]
    y = _dot((o * gate).astype(BF16), w_ref[...])
    x1 = x_ref[0] + g1_ref[0] * y
    x1_ref[0] = x1
    h2 = _rms_mod(x1, sc_ref[0], sh_ref[0])
    h2_ref[0] = _pack_bf16_pairs(h2)
    lg = _dot_split_weights(h2, wr_ref[...], LANES) + br_ref[...]
    meta, wt, counts = _route_tile(lg, tri_ref[...], counts_ref[...])
    meta_ref[0] = meta.T[:SUBLANES, :]
    wt_ref[0] = wt
    counts_ref[...] = counts


def _outproj(head_norm, o, gate, gain, w_out, x, g1, sc2, sh2, w_route, b_route):
    b, l, d = x.shape
    tm = min(ROW_TILE, l)
    row = lambda i, j: (i, j, 0)
    vec = lambda i, j: (i, 0, 0)
    const = lambda i, j: (0, 0)
    return pl.pallas_call(
        functools.partial(_outproj_kernel, head_norm),
        grid=(b, l // tm),
        in_specs=[
            pl.BlockSpec((1, tm, d), row),
            pl.BlockSpec((1, tm, d), row),
            pl.BlockSpec(gain.shape, const),
            pl.BlockSpec(w_out.shape, const),
            pl.BlockSpec((1, tm, d), row),
            pl.BlockSpec((1, 1, d), vec),
            pl.BlockSpec((1, 1, d), vec),
            pl.BlockSpec((1, 1, d), vec),
            pl.BlockSpec(w_route.shape, const),
            pl.BlockSpec(b_route.shape, const),
            pl.BlockSpec((tm, tm), const),
        ],
        out_specs=[pl.BlockSpec((1, tm, d), row), pl.BlockSpec((1, tm, d // 2), row),
                   pl.BlockSpec((1, SUBLANES, tm), lambda i, j: (i, 0, j)), pl.BlockSpec((1, tm, LANES), row),
                   pl.BlockSpec((1, LANES), const)],
        out_shape=[jax.ShapeDtypeStruct((b, l, d), F32), jax.ShapeDtypeStruct((b, l, d // 2), PACKED),
                   jax.ShapeDtypeStruct((b, SUBLANES, l), jnp.int32), jax.ShapeDtypeStruct((b, l, LANES), F32),
                   jax.ShapeDtypeStruct((1, LANES), F32)],
        compiler_params=_params(("arbitrary", "arbitrary")),
        name="outproj",
    )(o, gate, gain, w_out, x, g1, sc2, sh2, w_route, b_route, jnp.tril(jnp.ones((tm, tm), BF16), -1))


def _route_tile(lg, tri_strict, counts):
    lane_i = lax.broadcasted_iota(jnp.int32, lg.shape, 1)
    lane = lane_i.astype(F32)
    big = 1e9

    def first_argmax(v, vmax):
        return jnp.min(jnp.where(v == vmax, lane, big), axis=1, keepdims=True)

    is_group = lane < N_GROUPS
    gl = jnp.where(is_group, lg, NEG_BIG)
    g_max = jnp.max(gl, axis=1, keepdims=True)
    g_sum = jnp.sum(jnp.where(is_group, jnp.exp(gl - g_max), 0.0), axis=1, keepdims=True)
    g_p = 1.0 / g_sum
    g_idx = first_argmax(gl, g_max)
    lo = N_GROUPS + EXPERTS_PER_GROUP * g_idx
    el = jnp.where((lane >= lo) & (lane < lo + EXPERTS_PER_GROUP), lg, NEG_BIG)
    m1 = jnp.max(el, axis=1, keepdims=True)
    i1 = first_argmax(el, m1)
    el2 = jnp.where(lane == i1, NEG_BIG, el)
    m2 = jnp.max(el2, axis=1, keepdims=True)
    i2 = first_argmax(el2, m2)
    e = jnp.exp(m2 - m1)
    w1 = g_p / (1.0 + e)
    w2 = w1 * e
    e1 = i1 - N_GROUPS
    e2 = i2 - N_GROUPS
    hot1 = (lane == e1).astype(F32)
    hot2 = (lane == e2).astype(F32)
    hot = hot1 + hot2
    before = _dot(tri_strict, hot.astype(BF16)) + counts
    rank1 = jnp.sum(before * hot1, axis=1, keepdims=True)
    rank2 = jnp.sum(before * hot2, axis=1, keepdims=True)
    meta = jnp.where(lane_i == 0, e1, jnp.where(lane_i == 1, e2, jnp.where(lane_i == 2, rank1, rank2)))
    return (meta.astype(jnp.int32), jnp.where(lane_i == 0, w1, w2),
            counts + jnp.sum(hot, axis=0, keepdims=True))


def _index_windows(idx):
    n = idx.shape[0]
    return jnp.zeros((n // SC_WINDOW, LANES), jnp.int32).at[:, :SC_WINDOW].set(
        idx.reshape(n // SC_WINDOW, SC_WINDOW))


def _sc_mesh():
    return plsc.VectorSubcoreMesh(core_axis_name="core", subcore_axis_name="subcore")


def _scatter_rows(x, idx, n_rows):
    n_in, d = x.shape
    n_idx = idx.shape[0]

    def program(x_hbm, i_hbm, o_hbm):
        def window(x_vmem, i_vmem):
            pltpu.sync_copy(x_vmem, o_hbm.at[i_vmem.at[0, pl.ds(0, SC_WINDOW)]])

        pltpu.emit_pipeline(
            window,
            grid=(n_idx // SC_WINDOW,),
            in_specs=[pl.BlockSpec((SC_WINDOW, d), lambda i: (i % (n_in // SC_WINDOW), 0)),
                      pl.BlockSpec((1, LANES), lambda i: (i, 0))],
            out_specs=[],
            core_axis_name=("core", "subcore"),
            dimension_semantics=(pltpu.PARALLEL,),
        )(x_hbm, i_hbm)

    return pl.kernel(program, out_type=jax.ShapeDtypeStruct((n_rows, d), x.dtype), mesh=_sc_mesh(),
                     name="moe_scatter_rows")(x, _index_windows(idx))


def _gather_rows(x, idx):
    d = x.shape[1]
    n_idx = idx.shape[0]

    def program(x_hbm, i_hbm, o_hbm):
        def window(i_vmem, o_vmem):
            pltpu.sync_copy(x_hbm.at[i_vmem.at[0, pl.ds(0, SC_WINDOW)]], o_vmem)

        pltpu.emit_pipeline(
            window,
            grid=(n_idx // SC_WINDOW,),
            in_specs=[pl.BlockSpec((1, LANES), lambda i: (i, 0))],
            out_specs=[pl.BlockSpec((SC_WINDOW, d), lambda i: (i, 0))],
            core_axis_name=("core", "subcore"),
            dimension_semantics=(pltpu.PARALLEL,),
        )(i_hbm, o_hbm)

    return pl.kernel(program, out_type=jax.ShapeDtypeStruct((n_idx, d), x.dtype), mesh=_sc_mesh(),
                     name="moe_gather_rows")(x, _index_windows(idx))


def _gmm_kernel(first_ref, count_ref, xs_ref, wgu_ref, wd_ref, ys_ref, wgu_bf, wd_bf, xbuf, obuf, sem_in, sem_out):
    e = pl.program_id(0)
    last = pl.num_programs(0) - 1
    slots = xbuf.shape[0]
    tm = xbuf.shape[1]
    half = xbuf.shape[2]
    f = wd_bf.shape[0]
    n_tiles = count_ref[e]
    first = first_ref[e]
    total = first_ref[last] + count_ref[last]

    def rows(g):
        return pl.ds(pl.multiple_of(g * tm, tm), tm)

    def load(g):
        return pltpu.make_async_copy(xs_ref.at[rows(g)], xbuf.at[g % slots], sem_in.at[g % slots])

    def store(g):
        return pltpu.make_async_copy(obuf.at[g % slots], ys_ref.at[rows(g)], sem_out.at[g % slots])

    @pl.when(e == 0)
    def _():
        for g in range(slots - 1):
            @pl.when(g < total)
            def _():
                load(g).start()

    @pl.when(n_tiles > 0)
    def _():
        wgu_bf[...] = wgu_ref[0].astype(BF16)
        wd_bf[...] = wd_ref[0].astype(BF16)

    def tile(t, carry):
        g = first + t
        load(g).wait()

        @pl.when(g + slots - 1 < total)
        def _():
            load(g + slots - 1).start()

        @pl.when(g >= slots)
        def _():
            store(g - slots).wait()

        x_hi, x_lo = _unpack_bf16_pairs(xbuf[g % slots])
        hu = _dot(x_hi.astype(BF16), wgu_bf[:half, :]) + _dot(x_lo.astype(BF16), wgu_bf[half:, :])
        act = _silu(hu[:, :f]) * hu[:, f:]
        obuf[g % slots] = _pack_bf16_pairs(_dot(act.astype(BF16), wd_bf[...]))
        store(g).start()
        return carry

    lax.fori_loop(0, n_tiles, tile, 0)

    @pl.when(e == last)
    def _():
        for j in range(slots):
            g = total - slots + j

            @pl.when(g >= 0)
            def _():
                store(g).wait()


def _gmm(layer, tile_first, tile_count, xs, w_gate_up, w_down):
    n_rows, half = xs.shape
    d = 2 * half
    tm = MOE_TILE
    f2 = w_gate_up.shape[2]
    grid_spec = pltpu.PrefetchScalarGridSpec(
        num_scalar_prefetch=2,
        grid=(N_EXPERTS,),
        in_specs=[
            pl.BlockSpec(memory_space=pl.ANY),
            pl.BlockSpec((1, d, f2), lambda e, tf, tc: (layer * N_EXPERTS + e, 0, 0)),
            pl.BlockSpec((1, f2 // 2, d), lambda e, tf, tc: (layer * N_EXPERTS + e, 0, 0)),
        ],
        out_specs=pl.BlockSpec(memory_space=pl.ANY),
        scratch_shapes=[
            pltpu.VMEM((d, f2), BF16), pltpu.VMEM((f2 // 2, d), BF16),
            pltpu.VMEM((GMM_SLOTS, tm, half), PACKED), pltpu.VMEM((GMM_SLOTS, tm, half), PACKED),
            pltpu.SemaphoreType.DMA((GMM_SLOTS,)), pltpu.SemaphoreType.DMA((GMM_SLOTS,)),
        ],
    )
    return pl.pallas_call(
        _gmm_kernel,
        grid_spec=grid_spec,
        out_shape=jax.ShapeDtypeStruct((n_rows, half), PACKED),
        input_output_aliases={2: 0},
        compiler_params=_params(("arbitrary",)),
        name="moe_gmm",
    )(tile_first, tile_count, xs, w_gate_up, w_down)


def _combine_kernel(y0_ref, y1_ref, wt_ref, x_ref, g2_ref, fg_ref, o_ref):
    x2 = x_ref[...] + g2_ref[0] * _combine_rows(y0_ref[...], y1_ref[...], wt_ref[...])
    o_ref[...] = x2 * lax.rsqrt(jnp.mean(x2 * x2, axis=-1, keepdims=True) + EPS) * fg_ref[...]


def _combine_final(yg, wt, x1, g2, final_gain, tokens_per_seq):
    t, d = x1.shape
    tt = min(TOKEN_TILE, tokens_per_seq)
    nb = t // tt
    per_seq = tokens_per_seq // tt
    return pl.pallas_call(
        _combine_kernel,
        grid=(nb,),
        in_specs=[
            pl.BlockSpec((tt, d // 2), lambda i: (i, 0)),
            pl.BlockSpec((tt, d // 2), lambda i: (i + nb, 0)),
            pl.BlockSpec((tt, LANES), lambda i: (i, 0)),
            pl.BlockSpec((tt, d), lambda i: (i, 0)),
            pl.BlockSpec((1, 1, d), lambda i: (i // per_seq, 0, 0)),
            pl.BlockSpec((1, d), lambda i: (0, 0)),
        ],
        out_specs=pl.BlockSpec((tt, d), lambda i: (i, 0)),
        out_shape=jax.ShapeDtypeStruct((t, d), F32),
        compiler_params=_params(("arbitrary",)),
        name="moe_combine",
    )(yg, yg, wt, x1, g2, final_gain)


def _sorted_positions(meta, counts, tile):
    counts = counts[0, :N_EXPERTS].astype(jnp.int32)
    tiles_per = (counts + tile - 1) // tile
    tile_first = jnp.cumsum(tiles_per) - tiles_per
    choice_major = lambda a: jnp.transpose(a, (1, 0, 2)).reshape(TOP_K, -1)
    eid = choice_major(meta[:, :TOP_K, :])
    rank = choice_major(meta[:, TOP_K:2 * TOP_K, :])
    onehot = (eid[..., None] == jnp.arange(N_EXPERTS, dtype=jnp.int32)).astype(jnp.int32)
    pos = jnp.sum(onehot * (tile_first * tile), axis=-1) + rank
    return pos.astype(jnp.int32), tile_first.astype(jnp.int32), tiles_per.astype(jnp.int32)


def _sorted_rows(n_tokens, tile):
    return (TOP_K * n_tokens // tile + N_EXPERTS) * tile


def _hier_moe(layer, h2, meta, counts, w_gate_up, w_down):
    pos, tile_first, tile_count = _sorted_positions(meta, counts, MOE_TILE)
    pos = pos.reshape(-1)
    xs = _scatter_rows(h2, pos, _sorted_rows(h2.shape[0], MOE_TILE))
    ys = _gmm(layer, tile_first, tile_count, xs, w_gate_up, w_down)
    return _gather_rows(ys, pos)


def _hi_lo_cols(w, n):
    k, m = w.shape
    wp = jnp.zeros((k, n), F32).at[:, :m].set(w)
    hi = wp.astype(BF16)
    lo = (wp - hi.astype(F32)).astype(BF16)
    return jnp.concatenate([hi, lo], axis=1)


def _pad_row(v, n):
    return jnp.zeros((1, n), F32).at[0, :v.shape[0]].set(v)


def kernel(x, c, ada_w, ada_b, gdn_w_in, gdn_conv, gdn_a_log, gdn_dt_bias, gdn_norm, gdn_w_out,
           fox_w_in, fox_b_f, fox_q_norm, fox_k_norm, fox_w_out,
           moe_w_group, moe_b_group, moe_w_router, moe_b_router, moe_w_gate_up, moe_w_down,
           final_norm):
    b, l, d = x.shape
    depth = ada_w.shape[0]
    t = b * l
    ada = _ada(c, ada_w, ada_b)
    w_gate_up = moe_w_gate_up.reshape(depth * N_EXPERTS, d, moe_w_gate_up.shape[-1])
    w_down = moe_w_down.reshape(depth * N_EXPERTS, moe_w_down.shape[-2], d)
    final_gain = final_norm.reshape(1, d)

    x_src = x
    for i in range(depth):
        sh1, sc1, g1, sh2, sc2, g2 = [ada[i, :, s * d:(s + 1) * d].reshape(b, 1, d) for s in range(6)]
        j = i // 2
        w_route = _hi_lo_cols(jnp.concatenate([moe_w_group[i], moe_w_router[i]], axis=1), LANES)
        b_route = _pad_row(jnp.concatenate([moe_b_group[i], moe_b_router[i]]), LANES)
        if i % 2 == 0:
            w_in = gdn_w_in[j]
            q, k, v, gate, gb, *x_new = _gdn_inproj(
                x_src, sc1, sh1, w_in.astype(BF16), _hi_lo_cols(w_in[:, 4 * d:], LANES),
                gdn_conv[j], jnp.concatenate([_pad_row(gdn_a_log[j], LANES), _pad_row(gdn_dt_bias[j], LANES)]))
            x = x_new[0] if x_new else x_src
            o = _gdn_chunk(q, k, v, gb)
            gain = jnp.tile(gdn_norm[j], N_HEADS_GDN).reshape(1, d)
            x1, h2, meta, wt, counts = _outproj(True, o, gate, gain, gdn_w_out[j].astype(BF16), x, g1, sc2, sh2,
                                      w_route, b_route)
        else:
            w_in = fox_w_in[j]
            q, k, v, gate, cum_t, *x_new = _fox_inproj(
                x_src, sc1, sh1, w_in.astype(BF16), _hi_lo_cols(w_in[:, 4 * d:], LANES),
                _pad_row(fox_b_f[j], LANES))
            x = x_new[0] if x_new else x_src
            cum =cum_t[:, :N_HEADS_FOX, :].reshape(b, N_HEADS_FOX // 2, 2, l)
            q_gain = jnp.tile(fox_q_norm[j], 2).reshape(1, LANES)
            k_gain = jnp.tile(fox_k_norm[j], 2).reshape(1, LANES)
            o = _fox_attn(q, k, v, cum, q_gain, k_gain)
            x1, h2, meta, wt, counts = _outproj(False, o, gate, final_gain, fox_w_out[j].astype(BF16), x, g1, sc2, sh2,
                                      w_route, b_route)
        yg = _hier_moe(i, h2.reshape(t, d // 2), meta, counts, w_gate_up, w_down)
        x_src = (x1, yg, wt.reshape(t, LANES), g2)
    x1, yg, wt, g2 = x_src
    return _combine_final(yg, wt, x1.reshape(t, d), g2, final_gain, l).reshape(b, l, d)
```

```python
import functools

import jax
import jax.numpy as jnp
from jax import lax
from jax.experimental import pallas as pl
from jax.experimental.pallas import tpu as pltpu
from jax.experimental.pallas import tpu_sc as plsc

F32 = jnp.float32
BF16 = jnp.bfloat16
PACKED = jnp.int32

EPS = 1e-6
CHUNK = 64
N_HEADS_GDN = 8
HEAD_DIM_GDN = 128
N_HEADS_FOX = 16
HEAD_DIM_FOX = 64
CONV_K = 4
N_GROUPS = 4
EXPERTS_PER_GROUP = 8
N_EXPERTS = N_GROUPS * EXPERTS_PER_GROUP
TOP_K = 2

LANES = 128
SUBLANES = 8
VMEM_LIMIT_BYTES = 56 * 1024 * 1024

ROW_TILE = 512
ATTN_TILE = 512
ATTN_UNROLL = 2
GDN_BLOCK = 512
GDN_CHUNKS_PER_ITER = 4
GDN_HEADS_PACKED = 4
MOE_TILE = 256
GMM_SLOTS = 4
TOKEN_TILE = 512
SC_WINDOW = 64
NEG_BIG = -1e30
LOG2E = 1.4426950408889634


def _params(semantics):
    return pltpu.CompilerParams(dimension_semantics=semantics, vmem_limit_bytes=VMEM_LIMIT_BYTES)


def _sigmoid(x):
    return 0.5 * jnp.tanh(0.5 * x) + 0.5


def _silu(x):
    return x * _sigmoid(x)


def _softplus(x):
    return jnp.maximum(x, 0.0) + jnp.log(1.0 + jnp.exp(-jnp.abs(x)))


def _split2(a):
    hi = a.astype(BF16)
    lo = (a - hi.astype(F32)).astype(BF16)
    return hi, lo


def _split3(a):
    hi = a.astype(BF16)
    r = a - hi.astype(F32)
    mid = r.astype(BF16)
    lo = (r - mid.astype(F32)).astype(BF16)
    return hi, mid, lo


def _dot(a, b):
    return jnp.dot(a, b, preferred_element_type=F32)


def _dot_nt(a, b):
    return lax.dot_general(a, b, (((1,), (1,)), ((), ())), preferred_element_type=F32)


def _dot_split_weights(a, w_hi_lo, n):
    a_hi, a_lo = _split2(a)
    r = _dot(a_hi, w_hi_lo)
    return r[:, :n] + r[:, n:] + _dot(a_lo, w_hi_lo[:, :n])


def _cumsum_rows(tri_bf16, v):
    hi, mid, lo = _split3(v)
    r = _dot(tri_bf16, jnp.concatenate([hi, mid], axis=1))
    return r[:, :LANES] + r[:, LANES:] + _dot(tri_bf16, lo)


def _rms_mod(x, sc, sh):
    ms = jnp.mean(x * x, axis=-1, keepdims=True)
    return x * lax.rsqrt(ms + EPS) * (1.0 + sc) + sh


def _pack_bf16_pairs(a):
    n = a.shape[1] // 2
    bits = lax.bitcast_convert_type(a.astype(BF16).astype(F32), jnp.uint32)
    words = (bits[:, :n] & jnp.uint32(0xFFFF0000)) | (bits[:, n:] >> 16)
    return lax.bitcast_convert_type(words, PACKED)


def _unpack_bf16_pairs(p):
    words = lax.bitcast_convert_type(p, jnp.uint32)
    hi = lax.bitcast_convert_type(words & jnp.uint32(0xFFFF0000), F32)
    lo = lax.bitcast_convert_type(words << 16, F32)
    return hi, lo


def _lane_col(v, idx, lane):
    return jnp.sum(jnp.where(lane == idx, v, 0.0), axis=1, keepdims=True)


def _ada_kernel(c_ref, w_ref, b_ref, o_ref):
    c = c_ref[...]
    o_ref[0] = _dot(_silu(c).astype(BF16), w_ref[0].astype(BF16)) + b_ref[0]


def _ada(c, ada_w, ada_b):
    depth, d, n = ada_w.shape
    b = c.shape[0]
    tn = 1536
    return pl.pallas_call(
        _ada_kernel,
        grid=(depth, n // tn),
        in_specs=[
            pl.BlockSpec((b, d), lambda i, j: (0, 0)),
            pl.BlockSpec((1, d, tn), lambda i, j: (i, 0, j)),
            pl.BlockSpec((1, 1, tn), lambda i, j: (i, 0, j)),
        ],
        out_specs=pl.BlockSpec((1, b, tn), lambda i, j: (i, 0, j)),
        out_shape=jax.ShapeDtypeStruct((depth, b, n), F32),
        compiler_params=_params(("arbitrary", "arbitrary")),
        name="ada",
    )(c, ada_w, ada_b.reshape(depth, 1, n))


def _combine_rows(y0, y1, wt):
    lane = lax.broadcasted_iota(jnp.int32, wt.shape, 1)
    w0 = _lane_col(wt, 0, lane)
    w1 = _lane_col(wt, 1, lane)
    a_hi, a_lo = _unpack_bf16_pairs(y0)
    b_hi, b_lo = _unpack_bf16_pairs(y1)
    return jnp.concatenate([w0 * a_hi + w1 * b_hi, w0 * a_lo + w1 * b_lo], axis=1)


N_PENDING_REFS = 5


def _residual_tile(pending, refs):
    if not pending:
        return refs[0][0]
    x1_ref, y0_ref, y1_ref, wt_ref, g2_ref = refs
    return x1_ref[0] + g2_ref[0] * _combine_rows(y0_ref[...], y1_ref[...], wt_ref[...])


def _gdn_inproj_kernel(pending, *refs):
    nx = N_PENDING_REFS if pending else 1
    sc_ref, sh_ref, w_ref, wab_ref, conv_ref, prm_ref, q_ref, k_ref, v_ref, gate_ref, gb_ref = refs[nx:nx + 11]
    halo_ref = refs[-1]
    x = _residual_tile(pending, refs[:nx])
    if pending:
        refs[nx + 11][0] = x
    tm, d = x.shape
    h = _rms_mod(x, sc_ref[0], sh_ref[0])
    hb = h.astype(BF16)

    @pl.when(pl.program_id(1) == 0)
    def _():
        halo_ref[:, 0:SUBLANES, :] = jnp.zeros((3, SUBLANES, d), F32)

    for s, o_ref in enumerate((q_ref, k_ref, v_ref)):
        raw = _dot(hb, w_ref[:, s * d:(s + 1) * d])
        halo_ref[s, SUBLANES:SUBLANES + tm, :] = raw
        cw = conv_ref[:, s * d:(s + 1) * d]
        y = raw * cw[3:4]
        for j in range(CONV_K - 1):
            off = SUBLANES - (CONV_K - 1) + j
            y = y + halo_ref[s, off:off + tm, :] * cw[j:j + 1]
        halo_ref[s, 0:SUBLANES, :] = halo_ref[s, tm:tm + SUBLANES, :]
        y = _silu(y)
        if s < 2:
            scale = HEAD_DIM_GDN ** -0.5 if s == 0 else 1.0
            for hh in range(N_HEADS_GDN):
                seg = y[:, hh * HEAD_DIM_GDN:(hh + 1) * HEAD_DIM_GDN]
                inv = lax.rsqrt(jnp.sum(seg * seg, axis=-1, keepdims=True) + EPS) * scale
                o_ref[0, :, hh * HEAD_DIM_GDN:(hh + 1) * HEAD_DIM_GDN] = (seg * inv).astype(o_ref.dtype)
        else:
            o_ref[0] = y.astype(o_ref.dtype)

    gate_ref[0] = _silu(_dot(hb, w_ref[:, 3 * d:4 * d])).astype(gate_ref.dtype)

    ab = _dot_split_weights(h, wab_ref[...], LANES)
    log_alpha = -jnp.exp(prm_ref[0:1, :]) * _softplus(ab + prm_ref[1:2, :])
    lane = lax.broadcasted_iota(jnp.int32, ab.shape, 1)
    gb_ref[0] = jnp.where(lane < N_HEADS_GDN, log_alpha, _sigmoid(ab))


def _residual_operands(x_src, tm):
    row = lambda i, j: (i, j, 0)
    if not isinstance(x_src, tuple):
        b, l, d = x_src.shape
        return False, [x_src], [pl.BlockSpec((1, tm, d), row)], [], []
    x1, yg, wt, g2 = x_src
    b, l, d = x1.shape
    per_seq = l // tm
    n_tiles = b * per_seq
    specs = [
        pl.BlockSpec((1, tm, d), row),
        pl.BlockSpec((tm, d // 2), lambda i, j: (i * per_seq + j, 0)),
        pl.BlockSpec((tm, d // 2), lambda i, j: (i * per_seq + j + n_tiles, 0)),
        pl.BlockSpec((tm, LANES), lambda i, j: (i * per_seq + j, 0)),
        pl.BlockSpec((1, 1, d), lambda i, j: (i, 0, 0)),
    ]
    return (True, [x1, yg, yg, wt, g2], specs, [jax.ShapeDtypeStruct((b, l, d), F32)],
            [pl.BlockSpec((1, tm, d), row)])


def _gdn_inproj(x_src, sc, sh, w_main, w_ab, conv_w, prm):
    b, l, d = (x_src[0] if isinstance(x_src, tuple) else x_src).shape
    tm = min(ROW_TILE, l)
    pending, x_ops, x_specs, x_out_shape, x_out_spec = _residual_operands(x_src, tm)
    act = jax.ShapeDtypeStruct((b, l, d), BF16)
    row = lambda i, j: (i, j, 0)
    vec = lambda i, j: (i, 0, 0)
    const = lambda i, j: (0, 0)
    return pl.pallas_call(
        functools.partial(_gdn_inproj_kernel, pending),
        grid=(b, l // tm),
        in_specs=x_specs + [
            pl.BlockSpec((1, 1, d), vec),
            pl.BlockSpec((1, 1, d), vec),
            pl.BlockSpec(w_main.shape, const),
            pl.BlockSpec(w_ab.shape, const),
            pl.BlockSpec(conv_w.shape, const),
            pl.BlockSpec(prm.shape, const),
        ],
        out_specs=[pl.BlockSpec((1, tm, d), row)] * 4 + [pl.BlockSpec((1, tm, LANES), row)] + x_out_spec,
        out_shape=[act, act, act, act, jax.ShapeDtypeStruct((b, l, LANES), F32)] + x_out_shape,
        scratch_shapes=[pltpu.VMEM((3, tm + SUBLANES, d), F32)],
        compiler_params=_params(("arbitrary", "arbitrary")),
        name="gdn_inproj",
    )(*x_ops, sc, sh, w_main, w_ab, conv_w, prm)


def _block_diag(r, blk, n_blk):
    return jnp.concatenate([jnp.where(blk == b, r, jnp.zeros_like(r)) for b in range(n_blk)], axis=0)


def _mm_packed(lhs, rhs, blk, n_blk):
    m = lhs.shape[0]
    l_hi, l_lo = _split2(lhs)
    r_hi, r_lo = _split2(rhs)
    t = _dot(jnp.concatenate([l_hi, l_lo], axis=0), _block_diag(r_hi, blk, n_blk))
    return t[:m] + t[m:] + _dot(l_hi, _block_diag(r_lo, blk, n_blk))


def _unit_lower_inverse(a_strict, eye, blk, n_blk):
    n = a_strict[0].shape[0]
    levels = n.bit_length() - 1
    m = [-a for a in a_strict]
    x = [eye + mi for mi in m]
    p = [_mm_packed(mi, mi, blk, n_blk) for mi in m]
    for _ in range(levels - 2):
        r = [_mm_packed(jnp.concatenate([pi, xi], axis=0), pi, blk, n_blk) for pi, xi in zip(p, x)]
        x = [xi + ri[n:] for xi, ri in zip(x, r)]
        p = [ri[:n] for ri in r]
    return [xi + _mm_packed(xi, pi, blk, n_blk) for xi, pi in zip(x, p)]


def _gdn_chunk_kernel(q_ref, k_ref, v_ref, gb_ref, o_ref, s_ref):
    lb = q_ref.shape[1]
    c = CHUNK
    dk = HEAD_DIM_GDN
    nc = GDN_CHUNKS_PER_ITER
    heads = range(N_HEADS_GDN)
    sl = [slice(hh * dk, (hh + 1) * dk) for hh in heads]

    @pl.when(pl.program_id(1) == 0)
    def _():
        s_ref[...] = jnp.zeros(s_ref.shape, F32)

    pk = GDN_HEADS_PACKED
    groups = N_HEADS_GDN // pk
    tri = (lax.broadcasted_iota(jnp.int32, (c, c), 0) >= lax.broadcasted_iota(jnp.int32, (c, c), 1)).astype(BF16)
    lane = lax.broadcasted_iota(jnp.int32, (c, LANES), 1)
    side = lane // c
    row_p = lax.broadcasted_iota(jnp.int32, (c, pk * c), 0)
    lane_p = lax.broadcasted_iota(jnp.int32, (c, pk * c), 1)
    blk = lane_p // c
    col_p = lane_p - blk * c
    incl = row_p >= col_p
    strict = row_p > col_p
    diag = row_p == col_p
    eye = diag.astype(F32)
    blk_k = lax.broadcasted_iota(jnp.int32, (c, pk * dk), 1) // dk

    def per_head(cols, width):
        return jnp.concatenate([jnp.broadcast_to(col, (c, width)) for col in cols], axis=1)

    def body(it, carry):
        r0, q, kd_t, egl_last, rhs_pair, kq, decay, eg_cols = [], [], [], [], [], [], [], []
        for j in range(nc):
            r = pl.multiple_of((it * nc + j) * c, c)
            r0.append(r)
            gb = gb_ref[0, pl.ds(r, c), :]
            g = _cumsum_rows(tri, gb)
            g_last = g[c - 1:c, :]
            eg = jnp.exp(g)
            egl = jnp.exp(g_last - g)
            eg_last = jnp.exp(g_last)
            for gi in range(groups):
                hs = [gi * pk + b for b in range(pk)]
                cols = slice(gi * pk * dk, (gi + 1) * pk * dk)
                q4 = q_ref[0, pl.ds(r, c), cols].astype(F32)
                k4 = k_ref[0, pl.ds(r, c), cols].astype(F32)
                v4 = v_ref[0, pl.ds(r, c), cols].astype(F32)
                beta4 = per_head([_lane_col(gb, N_HEADS_GDN + hh, lane) for hh in hs], dk)
                eg4 = per_head([_lane_col(eg, hh, lane) for hh in hs], dk)
                egl4 = per_head([_lane_col(egl, hh, lane) for hh in hs], dk)
                kb4 = k4 * beta4
                k4b = k4.astype(BF16)
                k_diag = jnp.concatenate([jnp.where(blk_k == b, k4b, jnp.zeros_like(k4b)) for b in range(pk)],
                                         axis=0)
                kq.append(_dot_nt(jnp.concatenate([kb4, q4], axis=0).astype(BF16), k_diag))
                g_col = per_head([_lane_col(g, hh, lane) for hh in hs], c)
                g_row = jnp.sum(jnp.where(diag, g_col, 0.0), axis=0, keepdims=True)
                decay.append(jnp.where(incl, jnp.exp(jnp.where(incl, g_col - g_row, 0.0)), 0.0))
                vb4 = v4 * beta4
                kbe4 = kb4 * eg4
                qe4 = q4 * eg4
                kd4 = k4 * egl4
                for b, hh in enumerate(hs):
                    hsl = slice(b * dk, (b + 1) * dk)
                    q.append(qe4[:, hsl])
                    kd_t.append(kd4[:, hsl].T.astype(BF16))
                    egl_last.append(_lane_col(eg_last, hh, lane[0:1]))
                for p in range(pk // 2):
                    a = slice(2 * p * dk, (2 * p + 1) * dk)
                    bsl = slice((2 * p + 1) * dk, (2 * p + 2) * dk)
                    rhs_pair.append(jnp.concatenate(
                        [jnp.concatenate([vb4[:, a], kbe4[:, a]], axis=1),
                         jnp.concatenate([vb4[:, bsl], kbe4[:, bsl]], axis=1)], axis=0).astype(BF16))
        n_grp = len(kq)
        t_inv = _unit_lower_inverse([jnp.where(strict, kq[i][:c] * decay[i], 0.0) for i in range(n_grp)],
                                    eye, blk, pk)
        qk_p = [kq[i][c:] * decay[i] for i in range(n_grp)]

        def head_lhs(packed, i_grp, b):
            tile = packed[i_grp][:, (b // 2) * LANES:(b // 2 + 1) * LANES]
            return jnp.where(side == b % 2, tile, 0.0).astype(BF16)

        uw, qk = [], []
        for i_grp in range(n_grp):
            for b in range(pk):
                uw.append(_dot(head_lhs(t_inv, i_grp, b), rhs_pair[i_grp * (pk // 2) + b // 2]))
                qk.append(head_lhs(qk_p, i_grp, b))
        n = len(uw)
        wq = [jnp.concatenate([uw[i][:, dk:], q[i]], axis=0).astype(BF16) for i in range(n)]

        s = [s_ref[hh] for hh in heads]
        for j in range(nc):
            idx = [j * N_HEADS_GDN + hh for hh in heads]
            ws_qs = [_dot(wq[i], s[hh].astype(BF16)) for hh, i in zip(heads, idx)]
            v_new = [(uw[i][:, :dk] - ws_qs[hh][:c]).astype(BF16) for hh, i in zip(heads, idx)]
            for hh, i in zip(heads, idx):
                pair = jnp.concatenate([v_new[hh - hh % 2], v_new[hh - hh % 2 + 1]], axis=0)
                o = ws_qs[hh][c:] + _dot(qk[i], pair)
                o_ref[0, pl.ds(r0[j], c), sl[hh]] = o.astype(o_ref.dtype)
            s = [s[hh] * egl_last[i] + _dot(kd_t[i], v_new[hh]) for hh, i in zip(heads, idx)]
        for hh in heads:
            s_ref[hh] = s[hh]
        return carry

    lax.fori_loop(0, lb // (c * nc), body, 0)


def _gdn_chunk(q, k, v, gb):
    b, l, d = q.shape
    lb = min(GDN_BLOCK, l)
    row = lambda i, j: (i, j, 0)
    return pl.pallas_call(
        _gdn_chunk_kernel,
        grid=(b, l // lb),
        in_specs=[pl.BlockSpec((1, lb, d), row)] * 3 + [pl.BlockSpec((1, lb, LANES), row)],
        out_specs=pl.BlockSpec((1, lb, d), row),
        out_shape=jax.ShapeDtypeStruct((b, l, d), F32),
        scratch_shapes=[pltpu.VMEM((N_HEADS_GDN, HEAD_DIM_GDN, HEAD_DIM_GDN), F32)],
        compiler_params=_params(("arbitrary", "arbitrary")),
        name="gdn_chunk",
    )(q, k, v, gb)


def _fox_inproj_kernel(pending, *refs):
    nx = N_PENDING_REFS if pending else 1
    sc_ref, sh_ref, w_ref, wf_ref, bf_ref, q_ref, k_ref, v_ref, g_ref, cum_t_ref = refs[nx:nx + 10]
    carry_ref = refs[-1]
    x = _residual_tile(pending, refs[:nx])
    if pending:
        refs[nx + 10][0] = x
    tm, d = x.shape
    h = _rms_mod(x, sc_ref[0], sh_ref[0])
    hb = h.astype(BF16)
    for s, o_ref in enumerate((q_ref, k_ref, v_ref)):
        o_ref[0] = _dot(hb, w_ref[:, s * d:(s + 1) * d]).astype(o_ref.dtype)
    g_ref[0] = _sigmoid(_dot(hb, w_ref[:, 3 * d:4 * d])).astype(g_ref.dtype)

    @pl.when(pl.program_id(1) == 0)
    def _():
        carry_ref[...] = jnp.zeros(carry_ref.shape, F32)

    f_logit = _dot_split_weights(h, wf_ref[...], LANES) + bf_ref[...]
    log_f = -_softplus(-f_logit)
    row = lax.broadcasted_iota(jnp.int32, (tm, tm), 0)
    col = lax.broadcasted_iota(jnp.int32, (tm, tm), 1)
    cum = _cumsum_rows((row >= col).astype(BF16), log_f) + carry_ref[...]
    carry_ref[...] = cum[tm - 1:tm, :]
    cum_t_ref[0] = cum.T


def _fox_inproj(x_src, sc, sh, w_main, w_f, b_f):
    b, l, d = (x_src[0] if isinstance(x_src, tuple) else x_src).shape
    tm = min(ROW_TILE, l)
    pending, x_ops, x_specs, x_out_shape, x_out_spec = _residual_operands(x_src, tm)
    act = jax.ShapeDtypeStruct((b, l, d), BF16)
    row = lambda i, j: (i, j, 0)
    vec = lambda i, j: (i, 0, 0)
    const = lambda i, j: (0, 0)
    return pl.pallas_call(
        functools.partial(_fox_inproj_kernel, pending),
        grid=(b, l // tm),
        in_specs=x_specs + [
            pl.BlockSpec((1, 1, d), vec),
            pl.BlockSpec((1, 1, d), vec),
            pl.BlockSpec(w_main.shape, const),
            pl.BlockSpec(w_f.shape, const),
            pl.BlockSpec(b_f.shape, const),
        ],
        out_specs=([pl.BlockSpec((1, tm, d), row)] * 4 + [pl.BlockSpec((1, LANES, tm), lambda i, j: (i, 0, j))]
                   + x_out_spec),
        out_shape=[act, act, act, act, jax.ShapeDtypeStruct((b, LANES, l), F32)] + x_out_shape,
        scratch_shapes=[pltpu.VMEM((1, LANES), F32)],
        compiler_params=_params(("arbitrary", "arbitrary")),
        name="fox_inproj",
    )(*x_ops, sc, sh, w_main, w_f, b_f)


def _fox_attn_kernel(q_ref, k_ref, v_ref, cum_ref, qg_ref, kg_ref, o_ref, kn_ref, s_ref, mx_ref, acc_ref):
    tq = s_ref.shape[2]
    hd = HEAD_DIM_FOX
    lane = lax.broadcasted_iota(jnp.int32, (1, LANES), 1)
    first = lane < hd
    n_col = tq // LANES

    def head_norm(t, gain):
        sq = t * t
        s0 = jnp.sum(jnp.where(first, sq, 0.0), axis=1, keepdims=True)
        s1 = jnp.sum(jnp.where(first, 0.0, sq), axis=1, keepdims=True)
        ms = jnp.where(first, s0, s1) * (1.0 / hd)
        return t * lax.rsqrt(ms + EPS) * gain

    kn_ref[...] = head_norm(k_ref[0].astype(F32), kg_ref[...]).astype(kn_ref.dtype)

    def q_tile(qi, carry):
        q0 = qi * tq
        qn = head_norm(q_ref[0, pl.ds(q0, tq), :].astype(F32), qg_ref[...]) * (hd ** -0.5 * LOG2E)
        q2 = jnp.concatenate([jnp.where(first, qn, 0.0), jnp.where(first, 0.0, qn)], axis=0).astype(BF16)

        def scores(j, masked):
            k0 = j * tq if isinstance(j, int) else pl.multiple_of(j * tq, tq)
            ck = cum_ref[0, 0, :, pl.ds(k0, tq)] * LOG2E
            s = _dot_nt(q2, kn_ref[pl.ds(k0, tq), :])
            s = jnp.concatenate([s[:tq] - ck[0:1], s[tq:] - ck[1:2]], axis=0)
            if masked:
                r = lax.broadcasted_iota(jnp.int32, (tq, tq), 0)
                c = lax.broadcasted_iota(jnp.int32, (tq, tq), 1)
                keep = jnp.concatenate([r >= c, r >= c], axis=0)
                s = jnp.where(keep, s, NEG_BIG)
            s_ref[j] = s
            mx = mx_ref[...]
            for t in range(n_col):
                mx = jnp.maximum(mx, s[:, t * LANES:(t + 1) * LANES])
            mx_ref[...] = mx

        def scores_step(j, c):
            scores(j, False)
            return c

        mx_ref[...] = jnp.full(mx_ref.shape, NEG_BIG, F32)
        if qi > 0:
            lax.fori_loop(0, qi, scores_step, 0, unroll=min(qi, ATTN_UNROLL))
        scores(qi, True)
        mx_ref[...] = jnp.broadcast_to(jnp.max(mx_ref[...], axis=1, keepdims=True), mx_ref.shape)

        acc_ref[...] = jnp.zeros(acc_ref.shape, F32)
        ones = jnp.ones((tq, LANES), BF16)

        def weighted_sum(j, c):
            k0 = j * tq if isinstance(j, int) else pl.multiple_of(j * tq, tq)
            m = mx_ref[...]
            s = s_ref[j]
            p = jnp.concatenate([jnp.exp2(s[:, t * LANES:(t + 1) * LANES] - m) for t in range(n_col)], axis=1)
            acc_ref[...] += _dot(p.astype(BF16), jnp.concatenate([v_ref[0, pl.ds(k0, tq), :], ones], axis=1))
            return c

        lax.fori_loop(0, qi + 1, weighted_sum, 0, unroll=min(qi + 1, ATTN_UNROLL))

        acc = acc_ref[...]
        out = acc[:, :LANES] / acc[:, LANES:]
        o_ref[0, pl.ds(q0, tq), :] = jnp.where(first, out[:tq], out[tq:]).astype(o_ref.dtype)
        return carry

    for qi in range(q_ref.shape[1] // tq):
        q_tile(qi, 0)


def _fox_attn(q, k, v, cum, q_gain, k_gain):
    b, l, d = q.shape
    tq = min(ATTN_TILE, l)
    pairs = d // LANES
    return pl.pallas_call(
        _fox_attn_kernel,
        grid=(b, pairs),
        in_specs=[
            pl.BlockSpec((1, l, LANES), lambda i, p: (i, 0, p)),
            pl.BlockSpec((1, l, LANES), lambda i, p: (i, 0, p)),
            pl.BlockSpec((1, l, LANES), lambda i, p: (i, 0, p)),
            pl.BlockSpec((1, 1, 2, l), lambda i, p: (i, p, 0, 0)),
            pl.BlockSpec((1, LANES), lambda i, p: (0, 0)),
            pl.BlockSpec((1, LANES), lambda i, p: (0, 0)),
        ],
        out_specs=pl.BlockSpec((1, l, LANES), lambda i, p: (i, 0, p)),
        out_shape=jax.ShapeDtypeStruct((b, l, d), BF16),
        scratch_shapes=[
            pltpu.VMEM((l, LANES), BF16),
            pltpu.VMEM((l // tq, 2 * tq, tq), F32),
            pltpu.VMEM((2 * tq, LANES), F32),
            pltpu.VMEM((2 * tq, 2 * LANES), F32),
        ],
        compiler_params=_params(("arbitrary", "arbitrary")),
        name="fox_attn",
    )(q, k, v, cum, q_gain, k_gain)


def _outproj_kernel(head_norm, o_ref, gate_ref, gain_ref, w_ref, x_ref, g1_ref, sc_ref, sh_ref,
                    wr_ref, br_ref, x1_ref, h2_ref, meta_ref, wt_ref, counts_ref):
    tm = x_ref.shape[1]

    @pl.when((pl.program_id(0) == 0) & (pl.program_id(1) == 0))
    def _():
        counts_ref[...] = jnp.zeros(counts_ref.shape, F32)

    o = o_ref[0].astype(F32)
    gate = gate_ref[0].astype(F32)
    if head_norm:
        parts = []
        for hh in range(N_HEADS_GDN):
            seg = o[:, hh * HEAD_DIM_GDN:(hh + 1) * HEAD_DIM_GDN]
            parts.append(seg * lax.rsqrt(jnp.mean(seg * seg, axis=-1, keepdims=True) + EPS))
        o = jnp.concatenate(parts, axis=1) * gain_ref[...]
    y = _dot((o * gate).astype(BF16), w_ref[...])
    x1 = x_ref[0] + g1_ref[0] * y
    x1_ref[0] = x1
    h2 = _rms_mod(x1, sc_ref[0], sh_ref[0])
    h2_ref[0] = _pack_bf16_pairs(h2)
    lg = _dot_split_weights(h2, wr_ref[...], LANES) + br_ref[...]
    row = lax.broadcasted_iota(jnp.int32, (tm, tm), 0)
    col = lax.broadcasted_iota(jnp.int32, (tm, tm), 1)
    meta, wt, counts = _route_tile(lg, (row > col).astype(BF16), counts_ref[...])
    meta_ref[0] = meta.T[:SUBLANES, :]
    wt_ref[0] = wt
    counts_ref[...] = counts


def _outproj(head_norm, o, gate, gain, w_out, x, g1, sc2, sh2, w_route, b_route):
    b, l, d = x.shape
    tm = min(ROW_TILE, l)
    row = lambda i, j: (i, j, 0)
    vec = lambda i, j: (i, 0, 0)
    const = lambda i, j: (0, 0)
    return pl.pallas_call(
        functools.partial(_outproj_kernel, head_norm),
        grid=(b, l // tm),
        in_specs=[
            pl.BlockSpec((1, tm, d), row),
            pl.BlockSpec((1, tm, d), row),
            pl.BlockSpec(gain.shape, const),
            pl.BlockSpec(w_out.shape, const),
            pl.BlockSpec((1, tm, d), row),
            pl.BlockSpec((1, 1, d), vec),
            pl.BlockSpec((1, 1, d), vec),
            pl.BlockSpec((1, 1, d), vec),
            pl.BlockSpec(w_route.shape, const),
            pl.BlockSpec(b_route.shape, const),
        ],
        out_specs=[pl.BlockSpec((1, tm, d), row), pl.BlockSpec((1, tm, d // 2), row),
                   pl.BlockSpec((1, SUBLANES, tm), lambda i, j: (i, 0, j)), pl.BlockSpec((1, tm, LANES), row),
                   pl.BlockSpec((1, LANES), const)],
        out_shape=[jax.ShapeDtypeStruct((b, l, d), F32), jax.ShapeDtypeStruct((b, l, d // 2), PACKED),
                   jax.ShapeDtypeStruct((b, SUBLANES, l), jnp.int32), jax.ShapeDtypeStruct((b, l, LANES), F32),
                   jax.ShapeDtypeStruct((1, LANES), F32)],
        compiler_params=_params(("arbitrary", "arbitrary")),
        name="outproj",
    )(o, gate, gain, w_out, x, g1, sc2, sh2, w_route, b_route)


def _route_tile(lg, tri_strict, counts):
    lane_i = lax.broadcasted_iota(jnp.int32, lg.shape, 1)
    lane = lane_i.astype(F32)
    big = 1e9

    def first_argmax(v, vmax):
        return jnp.min(jnp.where(v == vmax, lane, big), axis=1, keepdims=True)

    is_group = lane < N_GROUPS
    gl = jnp.where(is_group, lg, NEG_BIG)
    g_max = jnp.max(gl, axis=1, keepdims=True)
    g_sum = jnp.sum(jnp.where(is_group, jnp.exp(gl - g_max), 0.0), axis=1, keepdims=True)
    g_p = 1.0 / g_sum
    g_idx = first_argmax(gl, g_max)
    lo = N_GROUPS + EXPERTS_PER_GROUP * g_idx
    el = jnp.where((lane >= lo) & (lane < lo + EXPERTS_PER_GROUP), lg, NEG_BIG)
    m1 = jnp.max(el, axis=1, keepdims=True)
    i1 = first_argmax(el, m1)
    el2 = jnp.where(lane == i1, NEG_BIG, el)
    m2 = jnp.max(el2, axis=1, keepdims=True)
    i2 = first_argmax(el2, m2)
    e = jnp.exp(m2 - m1)
    w1 = g_p / (1.0 + e)
    w2 = w1 * e
    e1 = i1 - N_GROUPS
    e2 = i2 - N_GROUPS
    hot1 = (lane == e1).astype(F32)
    hot2 = (lane == e2).astype(F32)
    hot = hot1 + hot2
    before = _dot(tri_strict, hot.astype(BF16)) + counts
    rank1 = jnp.sum(before * hot1, axis=1, keepdims=True)
    rank2 = jnp.sum(before * hot2, axis=1, keepdims=True)
    meta = jnp.where(lane_i == 0, e1, jnp.where(lane_i == 1, e2, jnp.where(lane_i == 2, rank1, rank2)))
    return (meta.astype(jnp.int32), jnp.where(lane_i == 0, w1, w2),
            counts + jnp.sum(hot, axis=0, keepdims=True))


def _index_windows(idx):
    n = idx.shape[0]
    return jnp.zeros((n // SC_WINDOW, LANES), jnp.int32).at[:, :SC_WINDOW].set(
        idx.reshape(n // SC_WINDOW, SC_WINDOW))


def _sc_mesh():
    return plsc.VectorSubcoreMesh(core_axis_name="core", subcore_axis_name="subcore")


def _scatter_rows(x, idx, n_rows):
    n_in, d = x.shape
    n_idx = idx.shape[0]

    def program(x_hbm, i_hbm, o_hbm):
        def window(x_vmem, i_vmem):
            pltpu.sync_copy(x_vmem, o_hbm.at[i_vmem.at[0, pl.ds(0, SC_WINDOW)]])

        pltpu.emit_pipeline(
            window,
            grid=(n_idx // SC_WINDOW,),
            in_specs=[pl.BlockSpec((SC_WINDOW, d), lambda i: (i % (n_in // SC_WINDOW), 0)),
                      pl.BlockSpec((1, LANES), lambda i: (i, 0))],
            out_specs=[],
            core_axis_name=("core", "subcore"),
            dimension_semantics=(pltpu.PARALLEL,),
        )(x_hbm, i_hbm)

    return pl.kernel(program, out_type=jax.ShapeDtypeStruct((n_rows, d), x.dtype), mesh=_sc_mesh(),
                     name="moe_scatter_rows")(x, _index_windows(idx))


def _gather_rows(x, idx):
    d = x.shape[1]
    n_idx = idx.shape[0]

    def program(x_hbm, i_hbm, o_hbm):
        def window(i_vmem, o_vmem):
            pltpu.sync_copy(x_hbm.at[i_vmem.at[0, pl.ds(0, SC_WINDOW)]], o_vmem)

        pltpu.emit_pipeline(
            window,
            grid=(n_idx // SC_WINDOW,),
            in_specs=[pl.BlockSpec((1, LANES), lambda i: (i, 0))],
            out_specs=[pl.BlockSpec((SC_WINDOW, d), lambda i: (i, 0))],
            core_axis_name=("core", "subcore"),
            dimension_semantics=(pltpu.PARALLEL,),
        )(i_hbm, o_hbm)

    return pl.kernel(program, out_type=jax.ShapeDtypeStruct((n_idx, d), x.dtype), mesh=_sc_mesh(),
                     name="moe_gather_rows")(x, _index_windows(idx))


def _gmm_kernel(first_ref, count_ref, xs_ref, wgu_ref, wd_ref, ys_ref, wgu_bf, wd_bf, xbuf, obuf, sem_in, sem_out):
    e = pl.program_id(0)
    last = pl.num_programs(0) - 1
    slots = xbuf.shape[0]
    tm = xbuf.shape[1]
    half = xbuf.shape[2]
    f = wd_bf.shape[0]
    n_tiles = count_ref[e]
    first = first_ref[e]
    total = first_ref[last] + count_ref[last]

    def rows(g):
        return pl.ds(pl.multiple_of(g * tm, tm), tm)

    def load(g):
        return pltpu.make_async_copy(xs_ref.at[rows(g)], xbuf.at[g % slots], sem_in.at[g % slots])

    def store(g):
        return pltpu.make_async_copy(obuf.at[g % slots], ys_ref.at[rows(g)], sem_out.at[g % slots])

    @pl.when(e == 0)
    def _():
        for g in range(slots - 1):
            @pl.when(g < total)
            def _():
                load(g).start()

    @pl.when(n_tiles > 0)
    def _():
        wgu_bf[...] = wgu_ref[0].astype(BF16)
        wd_bf[...] = wd_ref[0].astype(BF16)

    def tile(t, carry):
        g = first + t
        load(g).wait()

        @pl.when(g + slots - 1 < total)
        def _():
            load(g + slots - 1).start()

        @pl.when(g >= slots)
        def _():
            store(g - slots).wait()

        x_hi, x_lo = _unpack_bf16_pairs(xbuf[g % slots])
        hu = _dot(x_hi.astype(BF16), wgu_bf[:half, :]) + _dot(x_lo.astype(BF16), wgu_bf[half:, :])
        act = _silu(hu[:, :f]) * hu[:, f:]
        obuf[g % slots] = _pack_bf16_pairs(_dot(act.astype(BF16), wd_bf[...]))
        store(g).start()
        return carry

    lax.fori_loop(0, n_tiles, tile, 0)

    @pl.when(e == last)
    def _():
        for j in range(slots):
            g = total - slots + j

            @pl.when(g >= 0)
            def _():
                store(g).wait()


def _gmm(layer, tile_first, tile_count, xs, w_gate_up, w_down):
    n_rows, half = xs.shape
    d = 2 * half
    tm = MOE_TILE
    f2 = w_gate_up.shape[2]
    grid_spec = pltpu.PrefetchScalarGridSpec(
        num_scalar_prefetch=2,
        grid=(N_EXPERTS,),
        in_specs=[
            pl.BlockSpec(memory_space=pl.ANY),
            pl.BlockSpec((1, d, f2), lambda e, tf, tc: (layer * N_EXPERTS + e, 0, 0)),
            pl.BlockSpec((1, f2 // 2, d), lambda e, tf, tc: (layer * N_EXPERTS + e, 0, 0)),
        ],
        out_specs=pl.BlockSpec(memory_space=pl.ANY),
        scratch_shapes=[
            pltpu.VMEM((d, f2), BF16), pltpu.VMEM((f2 // 2, d), BF16),
            pltpu.VMEM((GMM_SLOTS, tm, half), PACKED), pltpu.VMEM((GMM_SLOTS, tm, half), PACKED),
            pltpu.SemaphoreType.DMA((GMM_SLOTS,)), pltpu.SemaphoreType.DMA((GMM_SLOTS,)),
        ],
    )
    return pl.pallas_call(
        _gmm_kernel,
        grid_spec=grid_spec,
        out_shape=jax.ShapeDtypeStruct((n_rows, half), PACKED),
        input_output_aliases={2: 0},
        compiler_params=_params(("arbitrary",)),
        name="moe_gmm",
    )(tile_first, tile_count, xs, w_gate_up, w_down)


def _combine_kernel(y0_ref, y1_ref, wt_ref, x_ref, g2_ref, fg_ref, o_ref):
    x2 = x_ref[...] + g2_ref[0] * _combine_rows(y0_ref[...], y1_ref[...], wt_ref[...])
    o_ref[...] = x2 * lax.rsqrt(jnp.mean(x2 * x2, axis=-1, keepdims=True) + EPS) * fg_ref[...]


def _combine_final(yg, wt, x1, g2, final_gain, tokens_per_seq):
    t, d = x1.shape
    tt = min(TOKEN_TILE, tokens_per_seq)
    nb = t // tt
    per_seq = tokens_per_seq // tt
    return pl.pallas_call(
        _combine_kernel,
        grid=(nb,),
        in_specs=[
            pl.BlockSpec((tt, d // 2), lambda i: (i, 0)),
            pl.BlockSpec((tt, d // 2), lambda i: (i + nb, 0)),
            pl.BlockSpec((tt, LANES), lambda i: (i, 0)),
            pl.BlockSpec((tt, d), lambda i: (i, 0)),
            pl.BlockSpec((1, 1, d), lambda i: (i // per_seq, 0, 0)),
            pl.BlockSpec((1, d), lambda i: (0, 0)),
        ],
        out_specs=pl.BlockSpec((tt, d), lambda i: (i, 0)),
        out_shape=jax.ShapeDtypeStruct((t, d), F32),
        compiler_params=_params(("arbitrary",)),
        name="moe_combine",
    )(yg, yg, wt, x1, g2, final_gain)


def _sorted_positions(meta, counts, tile):
    counts = counts[0, :N_EXPERTS].astype(jnp.int32)
    tiles_per = (counts + tile - 1) // tile
    tile_first = jnp.cumsum(tiles_per) - tiles_per
    choice_major = lambda a: jnp.transpose(a, (1, 0, 2)).reshape(TOP_K, -1)
    eid = choice_major(meta[:, :TOP_K, :])
    rank = choice_major(meta[:, TOP_K:2 * TOP_K, :])
    onehot = (eid[..., None] == jnp.arange(N_EXPERTS, dtype=jnp.int32)).astype(jnp.int32)
    pos = jnp.sum(onehot * (tile_first * tile), axis=-1) + rank
    return pos.astype(jnp.int32), tile_first.astype(jnp.int32), tiles_per.astype(jnp.int32)


def _sorted_rows(n_tokens, tile):
    return (TOP_K * n_tokens // tile + N_EXPERTS) * tile


def _hier_moe(layer, h2, meta, counts, w_gate_up, w_down):
    pos, tile_first, tile_count = _sorted_positions(meta, counts, MOE_TILE)
    pos = pos.reshape(-1)
    xs = _scatter_rows(h2, pos, _sorted_rows(h2.shape[0], MOE_TILE))
    ys = _gmm(layer, tile_first, tile_count, xs, w_gate_up, w_down)
    return _gather_rows(ys, pos)


def _hi_lo_cols(w, n):
    k, m = w.shape
    wp = jnp.zeros((k, n), F32).at[:, :m].set(w)
    hi = wp.astype(BF16)
    lo = (wp - hi.astype(F32)).astype(BF16)
    return jnp.concatenate([hi, lo], axis=1)


def _pad_row(v, n):
    return jnp.zeros((1, n), F32).at[0, :v.shape[0]].set(v)


def kernel(x, c, ada_w, ada_b, gdn_w_in, gdn_conv, gdn_a_log, gdn_dt_bias, gdn_norm, gdn_w_out,
           fox_w_in, fox_b_f, fox_q_norm, fox_k_norm, fox_w_out,
           moe_w_group, moe_b_group, moe_w_router, moe_b_router, moe_w_gate_up, moe_w_down,
           final_norm):
    b, l, d = x.shape
    depth = ada_w.shape[0]
    t = b * l
    ada = _ada(c, ada_w, ada_b)
    w_gate_up = moe_w_gate_up.reshape(depth * N_EXPERTS, d, moe_w_gate_up.shape[-1])
    w_down = moe_w_down.reshape(depth * N_EXPERTS, moe_w_down.shape[-2], d)
    final_gain = final_norm.reshape(1, d)

    x_src = x
    for i in range(depth):
        sh1, sc1, g1, sh2, sc2, g2 = [ada[i, :, s * d:(s + 1) * d].reshape(b, 1, d) for s in range(6)]
        j = i // 2
        w_route = _hi_lo_cols(jnp.concatenate([moe_w_group[i], moe_w_router[i]], axis=1), LANES)
        b_route = _pad_row(jnp.concatenate([moe_b_group[i], moe_b_router[i]]), LANES)
        if i % 2 == 0:
            w_in = gdn_w_in[j]
            q, k, v, gate, gb, *x_new = _gdn_inproj(
                x_src, sc1, sh1, w_in.astype(BF16), _hi_lo_cols(w_in[:, 4 * d:], LANES),
                gdn_conv[j], jnp.concatenate([_pad_row(gdn_a_log[j], LANES), _pad_row(gdn_dt_bias[j], LANES)]))
            x = x_new[0] if x_new else x_src
            o = _gdn_chunk(q, k, v, gb)
            gain = jnp.tile(gdn_norm[j], N_HEADS_GDN).reshape(1, d)
            x1, h2, meta, wt, counts = _outproj(True, o, gate, gain, gdn_w_out[j].astype(BF16), x, g1, sc2, sh2,
                                      w_route, b_route)
        else:
            w_in = fox_w_in[j]
            q, k, v, gate, cum_t, *x_new = _fox_inproj(
                x_src, sc1, sh1, w_in.astype(BF16), _hi_lo_cols(w_in[:, 4 * d:], LANES),
                _pad_row(fox_b_f[j], LANES))
            x = x_new[0] if x_new else x_src
            cum =cum_t[:, :N_HEADS_FOX, :].reshape(b, N_HEADS_FOX // 2, 2, l)
            q_gain = jnp.tile(fox_q_norm[j], 2).reshape(1, LANES)
            k_gain = jnp.tile(fox_k_norm[j], 2).reshape(1, LANES)
            o = _fox_attn(q, k, v, cum, q_gain, k_gain)
            x1, h2, meta, wt, counts = _outproj(False, o, gate, final_gain, fox_w_out[j].astype(BF16), x, g1, sc2, sh2,
                                      w_route, b_route)
        yg = _hier_moe(i, h2.reshape(t, d // 2), meta, counts, w_gate_up, w_down)
        x_src = (x1, yg, wt.reshape(t, LANES), g2)
    x1, yg, wt, g2 = x_src
    return _combine_final(yg, wt, x1.reshape(t, d), g2, final_gain, l).reshape(b, l, d)
```
